```python
import math
import jax, jax.numpy as jnp
from jax import lax
import numpy as np

D_MODEL = 1024
BATCH = 4
SEQ = 4096
DEPTH = 1

HEAD_DIM = 64
N_HEADS_A = D_MODEL // 128
N_HEADS_B = D_MODEL // 256
WIDTH_A = N_HEADS_A * HEAD_DIM
WIDTH_B = N_HEADS_B * 2 * HEAD_DIM
D_IN = 3 * WIDTH_A + 3 * WIDTH_B
D_FF = 11 * D_MODEL // 4
PLE_DIM = 256
GRID_W = 64
NA_KH = 8
NA_KW = 16
Q_BLOCK = 128
CONV_W = 3
EPS = 1e-6

kernel_name = "hybrid_natten_diffattn_convffn_ple"


def rmsnorm(x, g):
    xf = x.astype(jnp.float32)
    y = xf * lax.rsqrt(jnp.mean(xf * xf, axis=-1, keepdims=True) + EPS)
    return (y * g.astype(jnp.float32)).astype(x.dtype)


def neighbourhood_attention(q, k, v, rpb):
    B, S, H, dh = q.shape
    R = S // GRID_W
    kh = min(NA_KH, R)
    scale = 1.0 / math.sqrt(dh)

    def grid(t):
        return t.reshape(B, R, GRID_W, H, dh).transpose(0, 3, 1, 2, 4)

    qg, kg, vg = grid(q), grid(k), grid(v)
    r = jnp.arange(R)
    rs = jnp.clip(r - kh // 2, 0, R - kh)
    row_idx = rs[:, None] + jnp.arange(kh)[None, :]
    k_rows = kg[:, :, row_idx]
    v_rows = vg[:, :, row_idx]
    c = jnp.arange(GRID_W)
    cs = jnp.clip(c - NA_KW // 2, 0, GRID_W - NA_KW)
    col_mask = (c[None, :] >= cs[:, None]) & (c[None, :] < cs[:, None] + NA_KW)
    dr = row_idx - r[:, None] + (NA_KH - 1)
    dc = jnp.clip(c[None, :] - c[:, None], -(NA_KW - 1), NA_KW - 1) + (NA_KW - 1)
    bias = rpb[:, dr[:, None, :, None], dc[None, :, None, :]]
    s = jnp.einsum('bhrcd,bhrkwd->bhrckw', qg, k_rows).astype(jnp.float32) * scale
    s = s + bias.astype(jnp.float32)[None]
    s = jnp.where(col_mask[None, None, None, :, None, :], s, -jnp.inf)
    pr = jax.nn.softmax(s.reshape(B, H, R, GRID_W, kh * GRID_W), axis=-1)
    pr = pr.reshape(B, H, R, GRID_W, kh, GRID_W).astype(v.dtype)
    o = jnp.einsum('bhrckw,bhrkwd->bhrcd', pr, v_rows)
    return o.transpose(0, 2, 3, 1, 4).reshape(B, S, H * dh)


def diff_attention(q, k, v, lam):
    B, S, Hb, _, dh = q.shape
    nb = S // Q_BLOCK
    scale = 1.0 / math.sqrt(dh)
    kh = k.transpose(0, 2, 3, 1, 4)
    vh = v.transpose(0, 2, 1, 3)
    qb = q.transpose(0, 2, 3, 1, 4).reshape(B, Hb, 2, nb, Q_BLOCK, dh).transpose(3, 0, 1, 2, 4, 5)
    slopes = 2.0 ** (-8.0 * (jnp.arange(Hb, dtype=jnp.float32) + 1.0) / Hb)
    s_pos = jnp.arange(S)

    def block(args):
        qblk, start = args
        t = start + jnp.arange(Q_BLOCK)
        alibi = -slopes[:, None, None] * jnp.abs(t[:, None] - s_pos[None, :]).astype(jnp.float32)
        sc = jnp.einsum('bhcqd,bhckd->bhcqk', qblk, kh).astype(jnp.float32) * scale
        pr = jax.nn.softmax(sc + alibi[None, :, None], axis=-1)
        a = pr[:, :, 0] - lam * pr[:, :, 1]
        return jnp.einsum('bhqk,bhkd->bhqd', a.astype(v.dtype), vh)

    starts = jnp.arange(nb) * Q_BLOCK
    o = lax.map(block, (qb, starts))
    return o.transpose(1, 0, 3, 2, 4).reshape(B, S, Hb, 2 * dh)


def depthwise_conv(u, w, b):
    C = u.shape[-1]
    y = lax.conv_general_dilated(u, w[:, None, :].astype(u.dtype), window_strides=(1,),
                                 padding=((CONV_W // 2, CONV_W // 2),),
                                 dimension_numbers=('NWC', 'WIO', 'NWC'),
                                 feature_group_count=C)
    return y + b.astype(u.dtype)


def setup_inputs(seed: int = 0) -> dict:
    key = jax.random.key(seed)
    ks = jax.random.split(key, 32)
    L, D = DEPTH, D_MODEL
    nrm = lambda k, shape, s: jax.random.normal(k, shape, jnp.float32) * s
    gain = lambda k, shape: 1.0 + 0.02 * jax.random.normal(k, shape, jnp.float32)
    return {
        "x": nrm(ks[0], (BATCH, SEQ, D), 1.0),
        "p": nrm(ks[1], (L, BATCH, SEQ, PLE_DIM), 1.0),
        "norm_mix_g": gain(ks[2], (L, D)),
        "w_in": nrm(ks[3], (L, D, D_IN), D ** -0.5),
        "qn_a_q": gain(ks[4], (L, HEAD_DIM)),
        "qn_a_k": gain(ks[5], (L, HEAD_DIM)),
        "rpb": nrm(ks[6], (L, N_HEADS_A, 2 * NA_KH - 1, 2 * NA_KW - 1), 0.1),
        "qn_b_q": gain(ks[7], (L, HEAD_DIM)),
        "qn_b_k": gain(ks[8], (L, HEAD_DIM)),
        "lam_q1": nrm(ks[9], (L, HEAD_DIM), 0.1),
        "lam_k1": nrm(ks[10], (L, HEAD_DIM), 0.1),
        "lam_q2": nrm(ks[11], (L, HEAD_DIM), 0.1),
        "lam_k2": nrm(ks[12], (L, HEAD_DIM), 0.1),
        "subln_g": gain(ks[13], (L, 2 * HEAD_DIM)),
        "w_proj_a": nrm(ks[14], (L, WIDTH_A, D), WIDTH_A ** -0.5),
        "w_proj_b": nrm(ks[15], (L, WIDTH_B, D), WIDTH_B ** -0.5),
        "w_gate": nrm(ks[16], (L, D, 2 * D), D ** -0.5),
        "b_gate": nrm(ks[17], (L, 2 * D), 0.01),
        "w_out": nrm(ks[18], (L, D, D), D ** -0.5),
        "norm_ffn_g": gain(ks[19], (L, D)),
        "w_up": nrm(ks[20], (L, D, 2 * D_FF), D ** -0.5),
        "conv_w": nrm(ks[21], (L, CONV_W, 2 * D_FF), CONV_W ** -0.5),
        "conv_b": nrm(ks[22], (L, 2 * D_FF), 0.01),
        "w_down": nrm(ks[23], (L, D_FF, D), D_FF ** -0.5),
        "norm_ple_g": gain(ks[24], (L, D)),
        "w_ple_gate": nrm(ks[25], (L, D, D), D ** -0.5),
        "w_ple_proj": nrm(ks[26], (L, PLE_DIM, D), PLE_DIM ** -0.5),
    }


def reference(x, p, norm_mix_g, w_in, qn_a_q, qn_a_k, rpb, qn_b_q, qn_b_k,
              lam_q1, lam_k1, lam_q2, lam_k2, subln_g, w_proj_a, w_proj_b,
              w_gate, b_gate, w_out, norm_ffn_g, w_up, conv_w, conv_b, w_down,
              norm_ple_g, w_ple_gate, w_ple_proj):
    B, S, D = x.shape
    splits = [WIDTH_A, 2 * WIDTH_A, 3 * WIDTH_A, 3 * WIDTH_A + WIDTH_B, 3 * WIDTH_A + 2 * WIDTH_B]
    for i in range(DEPTH):
        h = rmsnorm(x, norm_mix_g[i])
        proj = h @ w_in[i]
        qa, ka, va, qb, kb, vb = jnp.split(proj, splits, axis=-1)
        qa = rmsnorm(qa.reshape(B, S, N_HEADS_A, HEAD_DIM), qn_a_q[i])
        ka = rmsnorm(ka.reshape(B, S, N_HEADS_A, HEAD_DIM), qn_a_k[i])
        va = va.reshape(B, S, N_HEADS_A, HEAD_DIM)
        ya = neighbourhood_attention(qa, ka, va, rpb[i])

        qb = rmsnorm(qb.reshape(B, S, N_HEADS_B, 2, HEAD_DIM), qn_b_q[i])
        kb = rmsnorm(kb.reshape(B, S, N_HEADS_B, 2, HEAD_DIM), qn_b_k[i])
        vb = vb.reshape(B, S, N_HEADS_B, 2 * HEAD_DIM)
        lam_init = 0.8 - 0.6 * math.exp(-0.3 * i)
        lam = (jnp.exp(jnp.sum(lam_q1[i].astype(jnp.float32) * lam_k1[i].astype(jnp.float32)))
               - jnp.exp(jnp.sum(lam_q2[i].astype(jnp.float32) * lam_k2[i].astype(jnp.float32)))
               + lam_init)
        yb = diff_attention(qb, kb, vb, lam)
        yb = (rmsnorm(yb, subln_g[i]) * (1.0 - lam_init)).reshape(B, S, WIDTH_B)

        g = jax.nn.sigmoid(h @ w_gate[i] + b_gate[i])
        ga, gb = jnp.split(g, 2, axis=-1)
        mixed = ga * (ya @ w_proj_a[i]) + gb * (yb @ w_proj_b[i])
        x = x + mixed @ w_out[i]

        h2 = rmsnorm(x, norm_ffn_g[i])
        u = depthwise_conv(h2 @ w_up[i], conv_w[i], conv_b[i])
        u_val, u_gate = jnp.split(u, 2, axis=-1)
        x = x + (jax.nn.gelu(u_gate, approximate=True) * u_val) @ w_down[i]

        pg = jax.nn.sigmoid(rmsnorm(x, norm_ple_g[i]) @ w_ple_gate[i])
        x = x + pg * (p[i] @ w_ple_proj[i])
    return x
```

```python
import functools
import math

import jax
import jax.numpy as jnp
import numpy as np
from jax import lax
from jax.experimental import pallas as pl
from jax.experimental.pallas import tpu as pltpu

F32 = jnp.float32
BF16 = jnp.bfloat16

HEAD_DIM = 64
GRID_W = 64
NA_KH = 8
NA_KW = 16
CONV_W = 3
EPS = 1e-6
LAM_INIT = 0.8 - 0.6 * math.exp(-0.3 * 0)
NEG_BIG = -1e30

LANES = 128
HALO = 8
VMEM_LIMIT = 56 * 1024 * 1024


def _params(n_axes, vmem=VMEM_LIMIT):
    return pltpu.CompilerParams(
        dimension_semantics=("arbitrary",) * n_axes, vmem_limit_bytes=vmem)


def _resident(shape):
    nd = len(shape)
    return pl.BlockSpec(shape, lambda *_: (0,) * nd, pipeline_mode=pl.Buffered(1))


def _rms_scale(x):
    return lax.rsqrt(jnp.mean(x * x, axis=-1, keepdims=True) + EPS)


def _bias_table_kernel(rpb_ref, e_ref):
    h = pl.program_id(0)
    n_dc = 2 * NA_KW - 1
    cq = lax.broadcasted_iota(jnp.int32, (GRID_W, LANES), 0)
    lane = lax.broadcasted_iota(jnp.int32, (GRID_W, LANES), 1)
    ck = lane % GRID_W
    upper = lane >= GRID_W
    dc = jnp.clip(ck - cq, -(NA_KW - 1), NA_KW - 1) + (NA_KW - 1)
    cs = jnp.clip(cq - NA_KW // 2, 0, GRID_W - NA_KW)
    inside = (ck >= cs) & (ck < cs + NA_KW)
    pairs = []
    for dr in range(2 * NA_KH - 2):
        acc = jnp.full((GRID_W, LANES), NEG_BIG, F32)
        for d in range(n_dc):
            lo = rpb_ref[h * (2 * NA_KH - 1) + dr, d]
            hi = rpb_ref[h * (2 * NA_KH - 1) + dr + 1, d]
            acc = jnp.where(inside & (dc == d), jnp.where(upper, hi, lo), acc)
        pairs.append(acc)
    for cls in range(NA_KH):
        for j in range(0, NA_KH, 2):
            e_ref[0, cls, :, j * GRID_W:(j + 2) * GRID_W] = pairs[cls + j]


def _bias_table(rpb):
    n_heads = rpb.shape[0]
    rpb2 = rpb.reshape(n_heads * (2 * NA_KH - 1), 2 * NA_KW - 1)
    return pl.pallas_call(
        _bias_table_kernel,
        grid=(n_heads,),
        in_specs=[pl.BlockSpec(memory_space=pltpu.SMEM)],
        out_specs=pl.BlockSpec((1, NA_KH, GRID_W, NA_KH * GRID_W), lambda h: (h, 0, 0, 0)),
        out_shape=jax.ShapeDtypeStruct((n_heads, NA_KH, GRID_W, NA_KH * GRID_W), F32),
        compiler_params=_params(1),
        name="bias_table",
    )(rpb2)


CHUNK = 512
SLAB = 256


def _inproj_kernel(x_ref, gmix_ref, w_ref, bgate_ref, gain_ref, bd_ref,
                   proj_ref, gate_ref, *, d_in, normed):
    x = x_ref[...]
    h = (x * _rms_scale(x) * gmix_ref[...]).astype(BF16)
    for c in range(d_in // CHUNK):
        p = jnp.dot(h, w_ref[:, c * CHUNK:(c + 1) * CHUNK], preferred_element_type=F32)
        if normed[c]:
            for s in range(CHUNK // SLAB):
                col = slice(c * CHUNK + s * SLAB, c * CHUNK + (s + 1) * SLAB)
                ps = p[:, s * SLAB:(s + 1) * SLAB]
                ms = jnp.dot((ps * ps).astype(BF16), bd_ref[...], preferred_element_type=F32)
                proj_ref[:, col] = (ps * lax.rsqrt(ms + EPS) * gain_ref[:, col]).astype(BF16)
        else:
            proj_ref[:, c * CHUNK:(c + 1) * CHUNK] = p.astype(BF16)
    d_gate = gate_ref.shape[1]
    for c in range(d_gate // CHUNK):
        col = slice(c * CHUNK, (c + 1) * CHUNK)
        z = jnp.dot(h, w_ref[:, d_in + c * CHUNK:d_in + (c + 1) * CHUNK],
                    preferred_element_type=F32) + bgate_ref[:, col]
        gate_ref[:, col] = jax.nn.sigmoid(z).astype(BF16)


def _inproj(x2, gmix, w_cat, b_gate, gain, bd, *, d_in, normed, tm):
    t, d = x2.shape
    d_gate = w_cat.shape[1] - d_in
    kern = functools.partial(_inproj_kernel, d_in=d_in, normed=normed)
    return pl.pallas_call(
        kern,
        grid=(t // tm,),
        in_specs=[
            pl.BlockSpec((tm, d), lambda i: (i, 0)),
            _resident(gmix.shape),
            _resident(w_cat.shape),
            _resident(b_gate.shape),
            _resident(gain.shape),
            _resident(bd.shape),
        ],
        out_specs=[
            pl.BlockSpec((tm, d_in), lambda i: (i, 0)),
            pl.BlockSpec((tm, d_gate), lambda i: (i, 0)),
        ],
        out_shape=[
            jax.ShapeDtypeStruct((t, d_in), BF16),
            jax.ShapeDtypeStruct((t, d_gate), BF16),
        ],
        compiler_params=_params(1),
        name="inproj",
    )(x2, gmix, w_cat, b_gate, gain, bd)


def _natten_kernel(q_ref, k_ref, v_ref, e_ref, o_ref, *, rb, n_rows):
    rblk = pl.program_id(2)
    lane = lax.broadcasted_iota(jnp.int32, (GRID_W, LANES), 1)
    first = lane < HEAD_DIM
    win = NA_KH * GRID_W

    def body(i, carry):
        r = rblk * rb + i
        rs = jnp.clip(r - NA_KH // 2, 0, n_rows - NA_KH)
        cls = rs - r + (NA_KH - 1)
        q = q_ref[pl.ds(pl.multiple_of(i * GRID_W, GRID_W), GRID_W), :]
        k0 = pl.multiple_of(rs * GRID_W, GRID_W)
        kw = k_ref[pl.ds(k0, win), :]
        vw = v_ref[pl.ds(k0, win), :]
        outs = []
        for hh in range(2):
            qm = jnp.where(first if hh == 0 else jnp.logical_not(first), q, jnp.zeros_like(q))
            s = lax.dot_general(qm, kw, (((1,), (1,)), ((), ())), preferred_element_type=F32)
            s = s + e_ref[hh, cls]
            m = jnp.max(s, axis=-1, keepdims=True)
            p = jnp.exp(s - m)
            l = jnp.sum(p, axis=-1, keepdims=True)
            outs.append(jnp.dot(p.astype(BF16), vw, preferred_element_type=F32) / l)
        o_ref[pl.ds(pl.multiple_of(i * GRID_W, GRID_W), GRID_W), :] = (
            jnp.where(first, outs[0], outs[1]).astype(BF16))
        return carry

    lax.fori_loop(0, rb, body, 0)


def _natten(proj, e_tab, *, batch, seq, n_pairs, q_blk, k_blk, v_blk, rb):
    n_rows = seq // GRID_W
    assert n_rows >= NA_KH and n_rows % rb == 0
    steps = n_rows // rb
    kern = functools.partial(_natten_kernel, rb=rb, n_rows=n_rows)
    return pl.pallas_call(
        kern,
        grid=(batch, n_pairs, steps),
        in_specs=[
            pl.BlockSpec((rb * GRID_W, LANES), lambda b, hp, r: (b * steps + r, q_blk + hp)),
            pl.BlockSpec((seq, LANES), lambda b, hp, r: (b, k_blk + hp)),
            pl.BlockSpec((seq, LANES), lambda b, hp, r: (b, v_blk + hp)),
            pl.BlockSpec((2, NA_KH, GRID_W, NA_KH * GRID_W), lambda b, hp, r: (hp, 0, 0, 0)),
        ],
        out_specs=pl.BlockSpec((rb * GRID_W, LANES), lambda b, hp, r: (b * steps + r, hp)),
        out_shape=jax.ShapeDtypeStruct((batch * seq, n_pairs * LANES), BF16),
        compiler_params=_params(3),
        name="natten",
    )(proj, proj, proj, e_tab)


def _diffattn_kernel(slopes_ref, lamp_ref, q_ref, k_ref, v_ref, subg_ref, o_ref,
                     m_sc, l_sc, acc_sc, *, tq, tk, seq):
    h = pl.program_id(1)
    qi = pl.program_id(2)
    slope = slopes_ref[h]
    q = q_ref[...]
    lane = lax.broadcasted_iota(jnp.int32, (tq, LANES), 1)
    first = lane < HEAD_DIM
    zero = jnp.zeros_like(q)
    qs = (jnp.where(first, q, zero), jnp.where(first, zero, q))
    rel = (lax.broadcasted_iota(jnp.int32, (tq, tk), 0)
           - lax.broadcasted_iota(jnp.int32, (tq, tk), 1)).astype(F32) * slope

    m_sc[...] = jnp.full(m_sc.shape, NEG_BIG, F32)
    l_sc[...] = jnp.zeros(l_sc.shape, F32)
    acc_sc[...] = jnp.zeros(acc_sc.shape, F32)

    def body(c, carry):
        k0 = pl.multiple_of(c * tk, tk)
        kc = k_ref[pl.ds(k0, tk), :]
        vc = v_ref[pl.ds(k0, tk), :]
        off = (qi * tq - c * tk).astype(F32) * slope
        bias = -jnp.abs(rel + off)
        for mi in range(2):
            s = lax.dot_general(qs[mi], kc, (((1,), (1,)), ((), ())),
                                preferred_element_type=F32) + bias
            m_prev = m_sc[mi]
            m_new = jnp.maximum(m_prev, jnp.max(s, axis=-1, keepdims=True))
            alpha = jnp.exp(m_prev - m_new)
            p = jnp.exp(s - m_new)
            l_sc[mi] = alpha * l_sc[mi] + jnp.sum(p, axis=-1, keepdims=True)
            acc_sc[mi] = alpha * acc_sc[mi] + jnp.dot(p.astype(BF16), vc,
                                                      preferred_element_type=F32)
            m_sc[mi] = m_new
        return carry

    lax.fori_loop(0, seq // tk, body, 0)

    lp = lamp_ref[...]
    lam = (jnp.exp(jnp.sum(lp[0:1] * lp[1:2], axis=-1, keepdims=True))
           - jnp.exp(jnp.sum(lp[2:3] * lp[3:4], axis=-1, keepdims=True)) + LAM_INIT)
    y = acc_sc[0] / l_sc[0] - lam * (acc_sc[1] / l_sc[1])
    y = y * _rms_scale(y) * subg_ref[...]
    o_ref[...] = (y * (1.0 - LAM_INIT)).astype(BF16)


def _diffattn(proj, slopes, lam_params, subg, *, batch, seq, n_heads, q_blk, k_blk, v_blk,
              tq, tk):
    steps = seq // tq
    kern = functools.partial(_diffattn_kernel, tq=tq, tk=tk, seq=seq)
    return pl.pallas_call(
        kern,
        grid=(batch, n_heads, steps),
        in_specs=[
            pl.BlockSpec(memory_space=pltpu.SMEM),
            pl.BlockSpec(lam_params.shape, lambda b, h, i: (0, 0)),
            pl.BlockSpec((tq, LANES), lambda b, h, i: (b * steps + i, q_blk + h)),
            pl.BlockSpec((seq, LANES), lambda b, h, i: (b, k_blk + h)),
            pl.BlockSpec((seq, LANES), lambda b, h, i: (b, v_blk + h)),
            pl.BlockSpec(subg.shape, lambda b, h, i: (0, 0)),
        ],
        out_specs=pl.BlockSpec((tq, LANES), lambda b, h, i: (b * steps + i, h)),
        out_shape=jax.ShapeDtypeStruct((batch * seq, n_heads * LANES), BF16),
        scratch_shapes=[
            pltpu.VMEM((2, tq, 1), F32),
            pltpu.VMEM((2, tq, 1), F32),
            pltpu.VMEM((2, tq, LANES), F32),
        ],
        compiler_params=_params(3),
        name="diffattn",
    )(slopes, lam_params, proj, proj, proj, subg)


def _mix_kernel(x_ref, ya_ref, yb_ref, g_ref, wa_ref, wb_ref, wo_ref, o_ref):
    d = x_ref.shape[1]
    pa = jnp.dot(ya_ref[...], wa_ref[...], preferred_element_type=F32)
    pb = jnp.dot(yb_ref[...], wb_ref[...], preferred_element_type=F32)
    mixed = g_ref[:, :d].astype(F32) * pa + g_ref[:, d:].astype(F32) * pb
    o_ref[...] = x_ref[...] + jnp.dot(mixed.astype(BF16), wo_ref[...],
                                      preferred_element_type=F32)


def _mix(x2, ya, yb, gate, wa, wb, wo, *, tm):
    t, d = x2.shape
    return pl.pallas_call(
        _mix_kernel,
        grid=(t // tm,),
        in_specs=[
            pl.BlockSpec((tm, d), lambda i: (i, 0)),
            pl.BlockSpec((tm, ya.shape[1]), lambda i: (i, 0)),
            pl.BlockSpec((tm, yb.shape[1]), lambda i: (i, 0)),
            pl.BlockSpec((tm, gate.shape[1]), lambda i: (i, 0)),
            _resident(wa.shape),
            _resident(wb.shape),
            _resident(wo.shape),
        ],
        out_specs=pl.BlockSpec((tm, d), lambda i: (i, 0)),
        out_shape=jax.ShapeDtypeStruct((t, d), F32),
        compiler_params=_params(1),
        name="mix",
    )(x2, ya, yb, gate, wa, wb, wo)


FF_CHUNK = 256


def _ffn_kernel(x_ref, xp_ref, xn_ref, p_ref, gffn_ref, wup_ref, cw_ref, cb_ref, wdn_ref,
                gple_ref, wpg_ref, wpp_ref, o_ref, acc_sc, *, tm, seq, d_ff):
    i = pl.program_id(0)
    tiles_per_seq = seq // tm
    pos = i % tiles_per_seq
    keep_prev = (pos != 0).astype(F32)
    keep_next = (pos != tiles_per_seq - 1).astype(F32)
    x = x_ref[...]
    g = gffn_ref[...]

    def norm(v):
        return v * _rms_scale(v) * g

    h = jnp.concatenate(
        [norm(xp_ref[...]) * keep_prev, norm(x), norm(xn_ref[...]) * keep_next],
        axis=0).astype(BF16)

    def conv(u, col):
        w = cw_ref[:, col]
        return (w[0:1] * u[HALO - 1:HALO - 1 + tm] + w[1:2] * u[HALO:HALO + tm]
                + w[2:3] * u[HALO + 1:HALO + 1 + tm] + cb_ref[:, col])

    for c in range(d_ff // FF_CHUNK):
        vcol = slice(c * FF_CHUNK, (c + 1) * FF_CHUNK)
        gcol = slice(d_ff + c * FF_CHUNK, d_ff + (c + 1) * FF_CHUNK)
        uv = conv(jnp.dot(h, wup_ref[:, vcol], preferred_element_type=F32), vcol)
        ug = conv(jnp.dot(h, wup_ref[:, gcol], preferred_element_type=F32), gcol)
        a = (jax.nn.gelu(ug, approximate=True) * uv).astype(BF16)
        part = jnp.dot(a, wdn_ref[vcol, :], preferred_element_type=F32)
        if c == 0:
            acc_sc[...] = part
        else:
            acc_sc[...] += part

    x2 = x + acc_sc[...]
    h3 = (x2 * _rms_scale(x2) * gple_ref[...]).astype(BF16)
    pg = jax.nn.sigmoid(jnp.dot(h3, wpg_ref[...], preferred_element_type=F32))
    pp = jnp.dot(p_ref[...].astype(BF16), wpp_ref[...], preferred_element_type=F32)
    o_ref[...] = x2 + pg * pp


def _ffn(x1, p2, gffn, wup, cw, cb, wdn, gple, wpg, wpp, *, seq, tm):
    t, d = x1.shape
    d_ff = wdn.shape[0]
    assert seq % tm == 0 and tm % HALO == 0 and d_ff % FF_CHUNK == 0
    hb = tm // HALO
    last = t // HALO - 1
    kern = functools.partial(_ffn_kernel, tm=tm, seq=seq, d_ff=d_ff)
    return pl.pallas_call(
        kern,
        grid=(t // tm,),
        in_specs=[
            pl.BlockSpec((tm, d), lambda i: (i, 0)),
            pl.BlockSpec((HALO, d), lambda i: (jnp.maximum(i * hb - 1, 0), 0)),
            pl.BlockSpec((HALO, d), lambda i: (jnp.minimum((i + 1) * hb, last), 0)),
            pl.BlockSpec((tm, p2.shape[1]), lambda i: (i, 0)),
            _resident(gffn.shape),
            _resident(wup.shape),
            _resident(cw.shape),
            _resident(cb.shape),
            _resident(wdn.shape),
            _resident(gple.shape),
            _resident(wpg.shape),
            _resident(wpp.shape),
        ],
        out_specs=pl.BlockSpec((tm, d), lambda i: (i, 0)),
        out_shape=jax.ShapeDtypeStruct((t, d), F32),
        scratch_shapes=[pltpu.VMEM((tm, d), F32)],
        compiler_params=_params(1),
        name="ffn",
    )(x1, x1, x1, p2, gffn, wup, cw, cb, wdn, gple, wpg, wpp)


def kernel(x, p, norm_mix_g, w_in, qn_a_q, qn_a_k, rpb, qn_b_q, qn_b_k, lam_q1, lam_k1, lam_q2, lam_k2, subln_g, w_proj_a, w_proj_b, w_gate, b_gate, w_out, norm_ffn_g, w_up, conv_w, conv_b, w_down, norm_ple_g, w_ple_gate, w_ple_proj):
    batch, seq, d = x.shape
    depth = p.shape[0]
    assert depth == 1
    t = batch * seq
    width_a = w_proj_a.shape[1]
    width_b = w_proj_b.shape[1]
    n_heads_a = width_a // HEAD_DIM
    n_heads_b = width_b // (2 * HEAD_DIM)
    d_in = 3 * width_a + 3 * width_b
    assert width_a == CHUNK and width_b == CHUNK

    scale = 1.0 / math.sqrt(HEAD_DIM)
    ones = jnp.ones((CHUNK,), F32)
    gain = jnp.concatenate([
        jnp.tile(qn_a_q[0], n_heads_a) * scale, jnp.tile(qn_a_k[0], n_heads_a), ones,
        jnp.tile(qn_b_q[0], 2 * n_heads_b) * scale, jnp.tile(qn_b_k[0], 2 * n_heads_b), ones,
    ])[None, :]
    normed = (True, True, False, True, True, False)
    bd = jnp.asarray(np.kron(np.eye(SLAB // HEAD_DIM), np.full((HEAD_DIM, HEAD_DIM), 1.0 / HEAD_DIM)),
                     BF16)
    w_cat = jnp.concatenate([w_in[0], w_gate[0]], axis=1).astype(BF16)

    x2 = x.reshape(t, d)
    proj, gate = _inproj(x2, norm_mix_g, w_cat, b_gate, gain, bd,
                         d_in=d_in, normed=normed, tm=512)

    e_tab = _bias_table(rpb[0])
    blk = CHUNK // LANES
    ya = _natten(proj, e_tab, batch=batch, seq=seq, n_pairs=n_heads_a // 2,
                 q_blk=0, k_blk=blk, v_blk=2 * blk, rb=8)

    slopes = jnp.asarray(2.0 ** (-8.0 * (np.arange(n_heads_b) + 1.0) / n_heads_b), F32)
    lam_params = jnp.concatenate([lam_q1, lam_k1, lam_q2, lam_k2], axis=0)
    yb = _diffattn(proj, slopes, lam_params, subln_g, batch=batch, seq=seq, n_heads=n_heads_b,
                   q_blk=3 * blk, k_blk=4 * blk, v_blk=5 * blk, tq=512, tk=512)

    x1 = _mix(x2, ya, yb, gate, w_proj_a[0].astype(BF16), w_proj_b[0].astype(BF16),
              w_out[0].astype(BF16), tm=512)

    out = _ffn(x1, p[0].reshape(t, -1), norm_ffn_g, w_up[0].astype(BF16), conv_w[0], conv_b,
               w_down[0].astype(BF16), norm_ple_g, w_ple_gate[0].astype(BF16),
               w_ple_proj[0].astype(BF16), seq=seq, tm=512)
    return out.reshape(batch, seq, d)
```

```python
import functools
import math

import jax
import jax.numpy as jnp
import numpy as np
from jax import lax
from jax.experimental import pallas as pl
from jax.experimental.pallas import tpu as pltpu

F32 = jnp.float32
BF16 = jnp.bfloat16

HEAD_DIM = 64
GRID_W = 64
NA_KH = 8
NA_KW = 16
CONV_W = 3
EPS = 1e-6
LAM_INIT = 0.8 - 0.6 * math.exp(-0.3 * 0)
NEG_BIG = -1e30
LOG2E = math.log2(math.e)

LANES = 128
HALO = 8
VMEM_LIMIT = 56 * 1024 * 1024


def _params(n_axes, vmem=VMEM_LIMIT):
    return pltpu.CompilerParams(
        dimension_semantics=("arbitrary",) * n_axes, vmem_limit_bytes=vmem)


def _resident(shape):
    nd = len(shape)
    return pl.BlockSpec(shape, lambda *_: (0,) * nd, pipeline_mode=pl.Buffered(1))


def _rms_scale(x):
    return lax.rsqrt(jnp.mean(x * x, axis=-1, keepdims=True) + EPS)


ROW_GROUP = 4
KEY_ROWS = 12
_GROUP_CLASSES = (
    (0, lambda i: 0),
    (-(NA_KH // 2), lambda i: i - NA_KH // 2),
    (ROW_GROUP - KEY_ROWS, lambda i: ROW_GROUP - NA_KH),
)


def _bias_table_kernel(rpb_ref, e_ref):
    h = pl.program_id(0)
    n_dr = 2 * NA_KH - 1
    n_dc = 2 * NA_KW - 1
    cq = lax.broadcasted_iota(jnp.int32, (GRID_W, LANES), 0)
    lane = lax.broadcasted_iota(jnp.int32, (GRID_W, LANES), 1)
    ck = lane % GRID_W
    upper = lane >= GRID_W
    dc = jnp.clip(ck - cq, -(NA_KW - 1), NA_KW - 1) + (NA_KW - 1)
    cs = jnp.clip(cq - NA_KW // 2, 0, GRID_W - NA_KW)
    inside = (ck >= cs) & (ck < cs + NA_KW)
    neg = jnp.full((GRID_W, LANES), NEG_BIG, F32)
    tiles = []
    for dr in range(n_dr):
        acc = neg
        for d in range(n_dc):
            acc = jnp.where(inside & (dc == d), rpb_ref[h * n_dr + dr, d], acc)
        tiles.append(acc)

    def half(ws_off, rs_off, i, j):
        row = ws_off + j
        if rs_off <= row < rs_off + NA_KH:
            return tiles[row - i + NA_KH - 1]
        return neg

    for cls, (ws_off, rs_fn) in enumerate(_GROUP_CLASSES):
        for i in range(ROW_GROUP):
            for j in range(0, KEY_ROWS, 2):
                lo = half(ws_off, rs_fn(i), i, j)
                hi = half(ws_off, rs_fn(i), i, j + 1)
                e_ref[0, cls, i * GRID_W:(i + 1) * GRID_W, j * GRID_W:(j + 2) * GRID_W] = (
                    jnp.where(upper, hi, lo))


def _bias_table(rpb):
    n_heads = rpb.shape[0]
    rpb2 = rpb.reshape(n_heads * (2 * NA_KH - 1), 2 * NA_KW - 1)
    shape = (n_heads, len(_GROUP_CLASSES), ROW_GROUP * GRID_W, KEY_ROWS * GRID_W)
    return pl.pallas_call(
        _bias_table_kernel,
        grid=(n_heads,),
        in_specs=[pl.BlockSpec(memory_space=pltpu.SMEM)],
        out_specs=pl.BlockSpec((1,) + shape[1:], lambda h: (h, 0, 0, 0)),
        out_shape=jax.ShapeDtypeStruct(shape, F32),
        compiler_params=_params(1),
        name="bias_table",
    )(rpb2)


CHUNK = 512
SLAB = 256


def _inproj_kernel(x_ref, gmix_ref, w_ref, bgate_ref, gain_ref, bd_ref,
                   proj_ref, gate_ref, *, d_in, normed):
    x = x_ref[...]
    h = (x * _rms_scale(x) * gmix_ref[...]).astype(BF16)
    for c in range(d_in // CHUNK):
        p = jnp.dot(h, w_ref[:, c * CHUNK:(c + 1) * CHUNK], preferred_element_type=F32)
        if normed[c]:
            for s in range(CHUNK // SLAB):
                col = slice(c * CHUNK + s * SLAB, c * CHUNK + (s + 1) * SLAB)
                ps = p[:, s * SLAB:(s + 1) * SLAB]
                ms = jnp.dot((ps * ps).astype(BF16), bd_ref[...], preferred_element_type=F32)
                proj_ref[:, col] = (ps * lax.rsqrt(ms + EPS) * gain_ref[:, col]).astype(BF16)
        else:
            proj_ref[:, c * CHUNK:(c + 1) * CHUNK] = p.astype(BF16)
    d_gate = gate_ref.shape[1]
    for c in range(d_gate // CHUNK):
        col = slice(c * CHUNK, (c + 1) * CHUNK)
        z = jnp.dot(h, w_ref[:, d_in + c * CHUNK:d_in + (c + 1) * CHUNK],
                    preferred_element_type=F32) + bgate_ref[:, col]
        gate_ref[:, col] = jax.nn.sigmoid(z).astype(BF16)


def _inproj(x2, gmix, w_cat, b_gate, gain, bd, *, d_in, normed, tm):
    t, d = x2.shape
    d_gate = w_cat.shape[1] - d_in
    kern = functools.partial(_inproj_kernel, d_in=d_in, normed=normed)
    return pl.pallas_call(
        kern,
        grid=(t // tm,),
        in_specs=[
            pl.BlockSpec((tm, d), lambda i: (i, 0)),
            _resident(gmix.shape),
            _resident(w_cat.shape),
            _resident(b_gate.shape),
            _resident(gain.shape),
            _resident(bd.shape),
        ],
        out_specs=[
            pl.BlockSpec((tm, d_in), lambda i: (i, 0)),
            pl.BlockSpec((tm, d_gate), lambda i: (i, 0)),
        ],
        out_shape=[
            jax.ShapeDtypeStruct((t, d_in), BF16),
            jax.ShapeDtypeStruct((t, d_gate), BF16),
        ],
        compiler_params=_params(1),
        name="inproj",
    )(x2, gmix, w_cat, b_gate, gain, bd)


def _natten_kernel(q_ref, k_ref, v_ref, e_ref, o_ref, *, groups, n_rows):
    gblk = pl.program_id(2)
    gq = ROW_GROUP * GRID_W
    win = KEY_ROWS * GRID_W
    lane = lax.broadcasted_iota(jnp.int32, (gq, LANES), 1)
    first = lane < HEAD_DIM
    for gi in range(groups):
        r0 = (gblk * groups + gi) * ROW_GROUP
        ws = jnp.clip(r0 - NA_KH // 2, 0, n_rows - KEY_ROWS)
        cls = jnp.where(r0 == 0, 0, jnp.where(r0 == n_rows - ROW_GROUP, 2, 1))
        q = q_ref[gi * gq:(gi + 1) * gq, :]
        k0 = pl.multiple_of(ws * GRID_W, GRID_W)
        kw = k_ref[pl.ds(k0, win), :]
        vw = v_ref[pl.ds(k0, win), :]
        outs = []
        for hh in range(2):
            qm = jnp.where(first if hh == 0 else jnp.logical_not(first), q, jnp.zeros_like(q))
            s = lax.dot_general(qm, kw, (((1,), (1,)), ((), ())), preferred_element_type=F32)
            s = s + e_ref[hh, cls]
            m = jnp.max(s, axis=-1, keepdims=True)
            p = jnp.exp(s - m)
            l = jnp.sum(p, axis=-1, keepdims=True)
            outs.append(jnp.dot(p.astype(BF16), vw, preferred_element_type=F32) / l)
        o_ref[gi * gq:(gi + 1) * gq, :] = jnp.where(first, outs[0], outs[1]).astype(BF16)


def _natten(proj, e_tab, *, batch, seq, n_pairs, q_blk, k_blk, v_blk, groups):
    n_rows = seq // GRID_W
    rb = groups * ROW_GROUP
    assert n_rows >= KEY_ROWS and n_rows % rb == 0
    steps = n_rows // rb
    kern = functools.partial(_natten_kernel, groups=groups, n_rows=n_rows)
    return pl.pallas_call(
        kern,
        grid=(n_pairs, batch, steps),
        in_specs=[
            pl.BlockSpec((rb * GRID_W, LANES), lambda hp, b, r: (b * steps + r, q_blk + hp)),
            pl.BlockSpec((seq, LANES), lambda hp, b, r: (b, k_blk + hp)),
            pl.BlockSpec((seq, LANES), lambda hp, b, r: (b, v_blk + hp)),
            pl.BlockSpec((2,) + e_tab.shape[1:], lambda hp, b, r: (hp, 0, 0, 0)),
        ],
        out_specs=pl.BlockSpec((rb * GRID_W, LANES), lambda hp, b, r: (b * steps + r, hp)),
        out_shape=jax.ShapeDtypeStruct((batch * seq, n_pairs * LANES), BF16),
        compiler_params=_params(3),
        name="natten",
    )(proj, proj, proj, e_tab)


def _diffattn_kernel(slopes_ref, lamp_ref, q_ref, k_ref, v_ref, subg_ref, o_ref,
                     s_buf, m_sc, l_sc, acc_sc, *, tq, tk, seq):
    h = pl.program_id(1)
    qi = pl.program_id(2)
    slope = slopes_ref[h]
    q = q_ref[...]
    lane = lax.broadcasted_iota(jnp.int32, (tq, LANES), 1)
    first = lane < HEAD_DIM
    zero = jnp.zeros_like(q)
    qs = (jnp.where(first, q, zero), jnp.where(first, zero, q))
    rel = (lax.broadcasted_iota(jnp.int32, (tq, tk), 0)
           - lax.broadcasted_iota(jnp.int32, (tq, tk), 1)).astype(F32) * slope
    n_chunks = seq // tk
    n_slabs = tk // LANES

    m_sc[...] = jnp.full(m_sc.shape, NEG_BIG, F32)
    l_sc[...] = jnp.zeros(l_sc.shape, F32)
    acc_sc[...] = jnp.zeros(acc_sc.shape, F32)

    def qk_body(c, carry):
        kc = k_ref[pl.ds(pl.multiple_of(c * tk, tk), tk), :]
        off = (qi * tq - c * tk).astype(F32) * slope
        bias = -jnp.abs(rel + off)
        for mi in range(2):
            s = lax.dot_general(qs[mi], kc, (((1,), (1,)), ((), ())),
                                preferred_element_type=F32) + bias
            s_buf[mi, c] = s
            m = m_sc[mi]
            for j in range(n_slabs):
                m = jnp.maximum(m, s[:, j * LANES:(j + 1) * LANES])
            m_sc[mi] = m
        return carry

    lax.fori_loop(0, n_chunks, qk_body, 0)

    for mi in range(2):
        m_sc[mi] = jnp.broadcast_to(jnp.max(m_sc[mi], axis=-1, keepdims=True), (tq, LANES))

    def pv_body(c, carry):
        vc = v_ref[pl.ds(pl.multiple_of(c * tk, tk), tk), :]
        for mi in range(2):
            m = m_sc[mi]
            s = s_buf[mi, c]
            lsum = l_sc[mi]
            ps = []
            for j in range(n_slabs):
                pj = jnp.exp2(s[:, j * LANES:(j + 1) * LANES] - m)
                lsum = lsum + pj
                ps.append(pj.astype(BF16))
            l_sc[mi] = lsum
            acc_sc[mi] += jnp.dot(jnp.concatenate(ps, axis=1), vc,
                                  preferred_element_type=F32)
        return carry

    lax.fori_loop(0, n_chunks, pv_body, 0)

    lp = lamp_ref[...]
    lam = (jnp.exp(jnp.sum(lp[0:1] * lp[1:2], axis=-1, keepdims=True))
           - jnp.exp(jnp.sum(lp[2:3] * lp[3:4], axis=-1, keepdims=True)) + LAM_INIT)
    ys = [acc_sc[mi] / jnp.sum(l_sc[mi], axis=-1, keepdims=True) for mi in range(2)]
    y = ys[0] - lam * ys[1]
    y = y * _rms_scale(y) * subg_ref[...]
    o_ref[...] = (y * (1.0 - LAM_INIT)).astype(BF16)


def _diffattn(proj, slopes, lam_params, subg, *, batch, seq, n_heads, q_blk, k_blk, v_blk,
              tq, tk):
    steps = seq // tq
    kern = functools.partial(_diffattn_kernel, tq=tq, tk=tk, seq=seq)
    return pl.pallas_call(
        kern,
        grid=(batch, n_heads, steps),
        in_specs=[
            pl.BlockSpec(memory_space=pltpu.SMEM),
            pl.BlockSpec(lam_params.shape, lambda b, h, i: (0, 0)),
            pl.BlockSpec((tq, LANES), lambda b, h, i: (b * steps + i, q_blk + h)),
            pl.BlockSpec((seq, LANES), lambda b, h, i: (b, k_blk + h)),
            pl.BlockSpec((seq, LANES), lambda b, h, i: (b, v_blk + h)),
            pl.BlockSpec(subg.shape, lambda b, h, i: (0, 0)),
        ],
        out_specs=pl.BlockSpec((tq, LANES), lambda b, h, i: (b * steps + i, h)),
        out_shape=jax.ShapeDtypeStruct((batch * seq, n_heads * LANES), BF16),
        scratch_shapes=[
            pltpu.VMEM((2, seq // tk, tq, tk), F32),
            pltpu.VMEM((2, tq, LANES), F32),
            pltpu.VMEM((2, tq, LANES), F32),
            pltpu.VMEM((2, tq, LANES), F32),
        ],
        compiler_params=_params(3),
        name="diffattn",
    )(slopes, lam_params, proj, proj, proj, subg)


def _mix_kernel(x_ref, ya_ref, yb_ref, g_ref, wa_ref, wb_ref, wo_ref, o_ref):
    d = x_ref.shape[1]
    pa = jnp.dot(ya_ref[...], wa_ref[...], preferred_element_type=F32)
    pb = jnp.dot(yb_ref[...], wb_ref[...], preferred_element_type=F32)
    mixed = g_ref[:, :d].astype(F32) * pa + g_ref[:, d:].astype(F32) * pb
    o_ref[...] = x_ref[...] + jnp.dot(mixed.astype(BF16), wo_ref[...],
                                      preferred_element_type=F32)


def _mix(x2, ya, yb, gate, wa, wb, wo, *, tm):
    t, d = x2.shape
    return pl.pallas_call(
        _mix_kernel,
        grid=(t // tm,),
        in_specs=[
            pl.BlockSpec((tm, d), lambda i: (i, 0)),
            pl.BlockSpec((tm, ya.shape[1]), lambda i: (i, 0)),
            pl.BlockSpec((tm, yb.shape[1]), lambda i: (i, 0)),
            pl.BlockSpec((tm, gate.shape[1]), lambda i: (i, 0)),
            _resident(wa.shape),
            _resident(wb.shape),
            _resident(wo.shape),
        ],
        out_specs=pl.BlockSpec((tm, d), lambda i: (i, 0)),
        out_shape=jax.ShapeDtypeStruct((t, d), F32),
        compiler_params=_params(1),
        name="mix",
    )(x2, ya, yb, gate, wa, wb, wo)


FF_CHUNK = 256


def _ffn_kernel(x_ref, xp_ref, xn_ref, p_ref, gffn_ref, wup_ref, cw_ref, cb_ref, wdn_ref,
                gple_ref, wpg_ref, wpp_ref, o_ref, acc_sc, *, tm, seq, d_ff):
    i = pl.program_id(0)
    tiles_per_seq = seq // tm
    pos = i % tiles_per_seq
    keep_prev = (pos != 0).astype(F32)
    keep_next = (pos != tiles_per_seq - 1).astype(F32)
    x = x_ref[...]
    g = gffn_ref[...]

    def norm(v):
        return v * _rms_scale(v) * g

    h = jnp.concatenate(
        [norm(xp_ref[...]) * keep_prev, norm(x), norm(xn_ref[...]) * keep_next],
        axis=0).astype(BF16)

    def conv(u, col):
        w = cw_ref[:, col]
        return (w[0:1] * u[HALO - 1:HALO - 1 + tm] + w[1:2] * u[HALO:HALO + tm]
                + w[2:3] * u[HALO + 1:HALO + 1 + tm] + cb_ref[:, col])

    for c in range(d_ff // FF_CHUNK):
        vcol = slice(c * FF_CHUNK, (c + 1) * FF_CHUNK)
        gcol = slice(d_ff + c * FF_CHUNK, d_ff + (c + 1) * FF_CHUNK)
        uv = conv(jnp.dot(h, wup_ref[:, vcol], preferred_element_type=F32), vcol)
        ug = conv(jnp.dot(h, wup_ref[:, gcol], preferred_element_type=F32), gcol)
        a = (jax.nn.gelu(ug, approximate=True) * uv).astype(BF16)
        part = jnp.dot(a, wdn_ref[vcol, :], preferred_element_type=F32)
        if c == 0:
            acc_sc[...] = part
        else:
            acc_sc[...] += part

    x2 = x + acc_sc[...]
    h3 = (x2 * _rms_scale(x2) * gple_ref[...]).astype(BF16)
    pg = jax.nn.sigmoid(jnp.dot(h3, wpg_ref[...], preferred_element_type=F32))
    pp = jnp.dot(p_ref[...].astype(BF16), wpp_ref[...], preferred_element_type=F32)
    o_ref[...] = x2 + pg * pp


def _ffn(x1, p2, gffn, wup, cw, cb, wdn, gple, wpg, wpp, *, seq, tm):
    t, d = x1.shape
    d_ff = wdn.shape[0]
    assert seq % tm == 0 and tm % HALO == 0 and d_ff % FF_CHUNK == 0
    hb = tm // HALO
    last = t // HALO - 1
    kern = functools.partial(_ffn_kernel, tm=tm, seq=seq, d_ff=d_ff)
    return pl.pallas_call(
        kern,
        grid=(t // tm,),
        in_specs=[
            pl.BlockSpec((tm, d), lambda i: (i, 0)),
            pl.BlockSpec((HALO, d), lambda i: (jnp.maximum(i * hb - 1, 0), 0)),
            pl.BlockSpec((HALO, d), lambda i: (jnp.minimum((i + 1) * hb, last), 0)),
            pl.BlockSpec((tm, p2.shape[1]), lambda i: (i, 0)),
            _resident(gffn.shape),
            _resident(wup.shape),
            _resident(cw.shape),
            _resident(cb.shape),
            _resident(wdn.shape),
            _resident(gple.shape),
            _resident(wpg.shape),
            _resident(wpp.shape),
        ],
        out_specs=pl.BlockSpec((tm, d), lambda i: (i, 0)),
        out_shape=jax.ShapeDtypeStruct((t, d), F32),
        scratch_shapes=[pltpu.VMEM((tm, d), F32)],
        compiler_params=_params(1),
        name="ffn",
    )(x1, x1, x1, p2, gffn, wup, cw, cb, wdn, gple, wpg, wpp)


def kernel(x, p, norm_mix_g, w_in, qn_a_q, qn_a_k, rpb, qn_b_q, qn_b_k, lam_q1, lam_k1, lam_q2, lam_k2, subln_g, w_proj_a, w_proj_b, w_gate, b_gate, w_out, norm_ffn_g, w_up, conv_w, conv_b, w_down, norm_ple_g, w_ple_gate, w_ple_proj):
    batch, seq, d = x.shape
    depth = p.shape[0]
    assert depth == 1
    t = batch * seq
    width_a = w_proj_a.shape[1]
    width_b = w_proj_b.shape[1]
    n_heads_a = width_a // HEAD_DIM
    n_heads_b = width_b // (2 * HEAD_DIM)
    d_in = 3 * width_a + 3 * width_b
    assert width_a == CHUNK and width_b == CHUNK

    scale = 1.0 / math.sqrt(HEAD_DIM)
    ones = jnp.ones((CHUNK,), F32)
    gain = jnp.concatenate([
        jnp.tile(qn_a_q[0], n_heads_a) * scale, jnp.tile(qn_a_k[0], n_heads_a), ones,
        jnp.tile(qn_b_q[0], 2 * n_heads_b) * (scale * LOG2E), jnp.tile(qn_b_k[0], 2 * n_heads_b), ones,
    ])[None, :]
    normed = (True, True, False, True, True, False)
    bd = jnp.asarray(np.kron(np.eye(SLAB // HEAD_DIM), np.full((HEAD_DIM, HEAD_DIM), 1.0 / HEAD_DIM)),
                     BF16)
    w_cat = jnp.concatenate([w_in[0], w_gate[0]], axis=1).astype(BF16)

    x2 = x.reshape(t, d)
    proj, gate = _inproj(x2, norm_mix_g, w_cat, b_gate, gain, bd,
                         d_in=d_in, normed=normed, tm=512)

    e_tab = _bias_table(rpb[0])
    blk = CHUNK // LANES
    ya = _natten(proj, e_tab, batch=batch, seq=seq, n_pairs=n_heads_a // 2,
                 q_blk=0, k_blk=blk, v_blk=2 * blk, groups=4)

    slopes = jnp.asarray(LOG2E * 2.0 ** (-8.0 * (np.arange(n_heads_b) + 1.0) / n_heads_b), F32)
    lam_params = jnp.concatenate([lam_q1, lam_k1, lam_q2, lam_k2], axis=0)
    yb = _diffattn(proj, slopes, lam_params, subln_g, batch=batch, seq=seq, n_heads=n_heads_b,
                   q_blk=3 * blk, k_blk=4 * blk, v_blk=5 * blk, tq=512, tk=512)

    x1 = _mix(x2, ya, yb, gate, w_proj_a[0].astype(BF16), w_proj_b[0].astype(BF16),
              w_out[0].astype(BF16), tm=512)

    out = _ffn(x1, p[0].reshape(t, -1), norm_ffn_g, w_up[0].astype(BF16), conv_w[0], conv_b,
               w_down[0].astype(BF16), norm_ple_g, w_ple_gate[0].astype(BF16),
               w_ple_proj[0].astype(BF16), seq=seq, tm=512)
    return out.reshape(batch, seq, d)
```

```python
import functools
import math

import jax
import jax.numpy as jnp
import numpy as np
from jax import lax
from jax.experimental import pallas as pl
from jax.experimental.pallas import tpu as pltpu

F32 = jnp.float32
BF16 = jnp.bfloat16

HEAD_DIM = 64
GRID_W = 64
NA_KH = 8
NA_KW = 16
CONV_W = 3
EPS = 1e-6
LAM_INIT = 0.8 - 0.6 * math.exp(-0.3 * 0)
NEG_BIG = -1e30
LOG2E = math.log2(math.e)

LANES = 128
HALO = 8
VMEM_LIMIT = 56 * 1024 * 1024


def _params(n_axes, vmem=VMEM_LIMIT):
    return pltpu.CompilerParams(
        dimension_semantics=("arbitrary",) * n_axes, vmem_limit_bytes=vmem)


def _resident(shape):
    nd = len(shape)
    return pl.BlockSpec(shape, lambda *_: (0,) * nd, pipeline_mode=pl.Buffered(1))


def _rms_scale(x):
    return lax.rsqrt(jnp.mean(x * x, axis=-1, keepdims=True) + EPS)


ROW_GROUP = 4
KEY_ROWS = 12
_GROUP_CLASSES = (
    (0, lambda i: 0),
    (-(NA_KH // 2), lambda i: i - NA_KH // 2),
    (ROW_GROUP - KEY_ROWS, lambda i: ROW_GROUP - NA_KH),
)


def _bias_table_kernel(rpb_ref, e_ref):
    h = pl.program_id(0)
    n_dr = 2 * NA_KH - 1
    n_dc = 2 * NA_KW - 1
    cq = lax.broadcasted_iota(jnp.int32, (GRID_W, LANES), 0)
    lane = lax.broadcasted_iota(jnp.int32, (GRID_W, LANES), 1)
    ck = lane % GRID_W
    upper = lane >= GRID_W
    dc = jnp.clip(ck - cq, -(NA_KW - 1), NA_KW - 1) + (NA_KW - 1)
    cs = jnp.clip(cq - NA_KW // 2, 0, GRID_W - NA_KW)
    inside = (ck >= cs) & (ck < cs + NA_KW)
    neg = jnp.full((GRID_W, LANES), NEG_BIG, F32)
    tiles = []
    for dr in range(n_dr):
        acc = neg
        for d in range(n_dc):
            acc = jnp.where(inside & (dc == d), rpb_ref[h * n_dr + dr, d], acc)
        tiles.append(acc)

    def half(ws_off, rs_off, i, j):
        row = ws_off + j
        if rs_off <= row < rs_off + NA_KH:
            return tiles[row - i + NA_KH - 1]
        return neg

    for cls, (ws_off, rs_fn) in enumerate(_GROUP_CLASSES):
        for i in range(ROW_GROUP):
            for j in range(0, KEY_ROWS, 2):
                lo = half(ws_off, rs_fn(i), i, j)
                hi = half(ws_off, rs_fn(i), i, j + 1)
                e_ref[0, cls, i * GRID_W:(i + 1) * GRID_W, j * GRID_W:(j + 2) * GRID_W] = (
                    jnp.where(upper, hi, lo))


def _bias_table(rpb):
    n_heads = rpb.shape[0]
    rpb2 = rpb.reshape(n_heads * (2 * NA_KH - 1), 2 * NA_KW - 1)
    shape = (n_heads, len(_GROUP_CLASSES), ROW_GROUP * GRID_W, KEY_ROWS * GRID_W)
    return pl.pallas_call(
        _bias_table_kernel,
        grid=(n_heads,),
        in_specs=[pl.BlockSpec(memory_space=pltpu.SMEM)],
        out_specs=pl.BlockSpec((1,) + shape[1:], lambda h: (h, 0, 0, 0)),
        out_shape=jax.ShapeDtypeStruct(shape, F32),
        compiler_params=_params(1),
        name="bias_table",
    )(rpb2)


CHUNK = 512
SLAB = 256


def _inproj_kernel(x_ref, gmix_ref, w_ref, bgate_ref, gain_ref, bd_ref,
                   proj_ref, gate_ref, *, d_in, normed):
    x = x_ref[...]
    h = (x * _rms_scale(x) * gmix_ref[...]).astype(BF16)
    for c in range(d_in // CHUNK):
        p = jnp.dot(h, w_ref[:, c * CHUNK:(c + 1) * CHUNK], preferred_element_type=F32)
        if normed[c]:
            for s in range(CHUNK // SLAB):
                col = slice(c * CHUNK + s * SLAB, c * CHUNK + (s + 1) * SLAB)
                ps = p[:, s * SLAB:(s + 1) * SLAB]
                ms = jnp.dot((ps * ps).astype(BF16), bd_ref[...], preferred_element_type=F32)
                proj_ref[:, col] = (ps * lax.rsqrt(ms + EPS) * gain_ref[:, col]).astype(BF16)
        else:
            proj_ref[:, c * CHUNK:(c + 1) * CHUNK] = p.astype(BF16)
    d_gate = gate_ref.shape[1]
    for c in range(d_gate // CHUNK):
        col = slice(c * CHUNK, (c + 1) * CHUNK)
        z = jnp.dot(h, w_ref[:, d_in + c * CHUNK:d_in + (c + 1) * CHUNK],
                    preferred_element_type=F32) + bgate_ref[:, col]
        gate_ref[:, col] = jax.nn.sigmoid(z).astype(BF16)


def _inproj(x2, gmix, w_cat, b_gate, gain, bd, *, d_in, normed, tm):
    t, d = x2.shape
    d_gate = w_cat.shape[1] - d_in
    kern = functools.partial(_inproj_kernel, d_in=d_in, normed=normed)
    return pl.pallas_call(
        kern,
        grid=(t // tm,),
        in_specs=[
            pl.BlockSpec((tm, d), lambda i: (i, 0)),
            _resident(gmix.shape),
            _resident(w_cat.shape),
            _resident(b_gate.shape),
            _resident(gain.shape),
            _resident(bd.shape),
        ],
        out_specs=[
            pl.BlockSpec((tm, d_in), lambda i: (i, 0)),
            pl.BlockSpec((tm, d_gate), lambda i: (i, 0)),
        ],
        out_shape=[
            jax.ShapeDtypeStruct((t, d_in), BF16),
            jax.ShapeDtypeStruct((t, d_gate), BF16),
        ],
        compiler_params=_params(1),
        name="inproj",
    )(x2, gmix, w_cat, b_gate, gain, bd)


def _natten_kernel(q_ref, k_ref, v_ref, e_ref, o_ref, *, groups, n_rows):
    gblk = pl.program_id(2)
    gq = ROW_GROUP * GRID_W
    win = KEY_ROWS * GRID_W
    lane = lax.broadcasted_iota(jnp.int32, (gq, LANES), 1)
    first = lane < HEAD_DIM
    for gi in range(groups):
        r0 = (gblk * groups + gi) * ROW_GROUP
        ws = jnp.clip(r0 - NA_KH // 2, 0, n_rows - KEY_ROWS)
        cls = jnp.where(r0 == 0, 0, jnp.where(r0 == n_rows - ROW_GROUP, 2, 1))
        q = q_ref[gi * gq:(gi + 1) * gq, :]
        k0 = pl.multiple_of(ws * GRID_W, GRID_W)
        kw = k_ref[pl.ds(k0, win), :]
        vw = v_ref[pl.ds(k0, win), :]
        outs = []
        for hh in range(2):
            qm = jnp.where(first if hh == 0 else jnp.logical_not(first), q, jnp.zeros_like(q))
            s = lax.dot_general(qm, kw, (((1,), (1,)), ((), ())), preferred_element_type=F32)
            s = s + e_ref[hh, cls]
            m = jnp.max(s, axis=-1, keepdims=True)
            p = jnp.exp(s - m)
            l = jnp.sum(p, axis=-1, keepdims=True)
            outs.append(jnp.dot(p.astype(BF16), vw, preferred_element_type=F32) / l)
        o_ref[gi * gq:(gi + 1) * gq, :] = jnp.where(first, outs[0], outs[1]).astype(BF16)


def _natten(proj, e_tab, *, batch, seq, n_pairs, q_blk, k_blk, v_blk, groups):
    n_rows = seq // GRID_W
    rb = groups * ROW_GROUP
    assert n_rows >= KEY_ROWS and n_rows % rb == 0
    steps = n_rows // rb
    kern = functools.partial(_natten_kernel, groups=groups, n_rows=n_rows)
    return pl.pallas_call(
        kern,
        grid=(n_pairs, batch, steps),
        in_specs=[
            pl.BlockSpec((rb * GRID_W, LANES), lambda hp, b, r: (b * steps + r, q_blk + hp)),
            pl.BlockSpec((seq, LANES), lambda hp, b, r: (b, k_blk + hp)),
            pl.BlockSpec((seq, LANES), lambda hp, b, r: (b, v_blk + hp)),
            pl.BlockSpec((2,) + e_tab.shape[1:], lambda hp, b, r: (hp, 0, 0, 0)),
        ],
        out_specs=pl.BlockSpec((rb * GRID_W, LANES), lambda hp, b, r: (b * steps + r, hp)),
        out_shape=jax.ShapeDtypeStruct((batch * seq, n_pairs * LANES), BF16),
        compiler_params=_params(3),
        name="natten",
    )(proj, proj, proj, e_tab)


POS_RADIX = 64
N_POS_FEATS = 8


def _key_pos_feats(seq):
    j = np.arange(seq)
    hi = -(POS_RADIX * (j // POS_RADIX)).astype(np.float32)
    lo = -(j % POS_RADIX).astype(np.float32)
    one = np.ones(seq, np.float32)
    feats = np.stack([one, one, one, one, hi, hi, lo, lo], axis=1)
    half = np.zeros((seq, HEAD_DIM), np.float32)
    half[:, :N_POS_FEATS] = feats
    return jnp.asarray(np.concatenate([half, half], axis=1), BF16)


def _query_pos_feats(row0, slope, tq):
    lane = lax.broadcasted_iota(jnp.int32, (tq, LANES), 1) % HEAD_DIM
    t = row0 + lax.broadcasted_iota(jnp.int32, (tq, LANES), 0)
    a = (t // POS_RADIX).astype(F32) * (slope * POS_RADIX)
    b = (t % POS_RADIX).astype(F32) * slope
    s = jnp.full((tq, LANES), slope, F32)

    def split(x):
        hi = x.astype(BF16).astype(F32)
        return hi, x - hi

    vals = split(a) + split(b) + split(s) + split(s)
    out = jnp.zeros((tq, LANES), F32)
    for f, v in enumerate(vals):
        out = jnp.where(lane == f, v, out)
    return out


def _diffattn_kernel(slopes_ref, lamp_ref, kf_ref, q_ref, k_ref, v_ref, subg_ref, o_ref,
                     *scratch, tq, tk, seq, subs):
    s_bufs, m_scs, l_scs, acc_scs = (scratch[i * subs:(i + 1) * subs] for i in range(4))
    h = pl.program_id(1)
    step = pl.program_id(2)
    slope = slopes_ref[h]
    lane = lax.broadcasted_iota(jnp.int32, (tq, LANES), 1)
    first = lane < HEAD_DIM
    n_chunks = seq // tk
    n_slabs = tk // LANES
    diag_bias = -jnp.abs((lax.broadcasted_iota(jnp.int32, (tq, tk), 0)
                          - lax.broadcasted_iota(jnp.int32, (tq, tk), 1)).astype(F32) * slope)

    def scores(u, c, qs, bias):
        k0 = pl.multiple_of(c * tk, tk)
        kc = k_ref[pl.ds(k0, tk), :]
        kf = kf_ref[pl.ds(k0, tk), :]
        ks = (jnp.where(first, kc, kf), jnp.where(first, kf, kc))
        for mi in range(2):
            s = lax.dot_general(qs[mi], ks[mi], (((1,), (1,)), ((), ())),
                                preferred_element_type=F32)
            if bias is not None:
                s = s + bias
            s_bufs[u][mi, c] = s
            m = m_scs[u][mi]
            for j in range(n_slabs):
                m = jnp.maximum(m, s[:, j * LANES:(j + 1) * LANES])
            m_scs[u][mi] = m

    for u in range(subs):
        qi = step * subs + u
        q = q_ref[u * tq:(u + 1) * tq, :]
        qf = _query_pos_feats(qi * tq, slope, tq)
        zero = jnp.zeros_like(q)
        q_diag = (jnp.where(first, q, zero), jnp.where(first, zero, q))
        q_after = (jnp.where(first, q, qf.astype(BF16)), jnp.where(first, qf.astype(BF16), q))
        q_before = (jnp.where(first, q, (-qf).astype(BF16)),
                    jnp.where(first, (-qf).astype(BF16), q))
        m_scs[u][...] = jnp.full(m_scs[u].shape, NEG_BIG, F32)
        l_scs[u][...] = jnp.zeros(l_scs[u].shape, F32)
        acc_scs[u][...] = jnp.zeros(acc_scs[u].shape, F32)

        scores(u, qi, q_diag, diag_bias)
        for d in range(1, n_chunks):
            c = lax.rem(qi + d, n_chunks)
            after = c > qi
            qs = tuple(jnp.where(after, q_after[mi], q_before[mi]) for mi in range(2))
            scores(u, c, qs, None)
        for mi in range(2):
            m_scs[u][mi] = jnp.broadcast_to(
                jnp.max(m_scs[u][mi], axis=-1, keepdims=True), (tq, LANES))

    for u in range(subs):
        for c in range(n_chunks):
            vc = v_ref[c * tk:(c + 1) * tk, :]
            for mi in range(2):
                m = m_scs[u][mi]
                s = s_bufs[u][mi, c]
                lsum = l_scs[u][mi]
                ps = []
                for j in range(n_slabs):
                    pj = jnp.exp2(s[:, j * LANES:(j + 1) * LANES] - m)
                    lsum = lsum + pj
                    ps.append(pj.astype(BF16))
                l_scs[u][mi] = lsum
                acc_scs[u][mi] += jnp.dot(jnp.concatenate(ps, axis=1), vc,
                                          preferred_element_type=F32)

    lp = lamp_ref[...]
    lam = (jnp.exp(jnp.sum(lp[0:1] * lp[1:2], axis=-1, keepdims=True))
           - jnp.exp(jnp.sum(lp[2:3] * lp[3:4], axis=-1, keepdims=True)) + LAM_INIT)
    for u in range(subs):
        ys = [acc_scs[u][mi] / jnp.sum(l_scs[u][mi], axis=-1, keepdims=True) for mi in range(2)]
        y = ys[0] - lam * ys[1]
        y = y * _rms_scale(y) * subg_ref[...]
        o_ref[u * tq:(u + 1) * tq, :] = (y * (1.0 - LAM_INIT)).astype(BF16)


def _diffattn(proj, slopes, lam_params, subg, *, batch, seq, n_heads, q_blk, k_blk, v_blk,
              tq, tk, subs):
    rows = subs * tq
    steps = seq // rows
    assert tq == tk and seq % rows == 0 and seq <= POS_RADIX * 256
    kfeat = _key_pos_feats(seq)
    kern = functools.partial(_diffattn_kernel, tq=tq, tk=tk, seq=seq, subs=subs)
    return pl.pallas_call(
        kern,
        grid=(batch, n_heads, steps),
        in_specs=[
            pl.BlockSpec(memory_space=pltpu.SMEM),
            pl.BlockSpec(lam_params.shape, lambda b, h, i: (0, 0)),
            _resident(kfeat.shape),
            pl.BlockSpec((rows, LANES), lambda b, h, i: (b * steps + i, q_blk + h)),
            pl.BlockSpec((seq, LANES), lambda b, h, i: (b, k_blk + h)),
            pl.BlockSpec((seq, LANES), lambda b, h, i: (b, v_blk + h)),
            pl.BlockSpec(subg.shape, lambda b, h, i: (0, 0)),
        ],
        out_specs=pl.BlockSpec((rows, LANES), lambda b, h, i: (b * steps + i, h)),
        out_shape=jax.ShapeDtypeStruct((batch * seq, n_heads * LANES), BF16),
        scratch_shapes=(
            [pltpu.VMEM((2, seq // tk, tq, tk), F32)] * subs
            + [pltpu.VMEM((2, tq, LANES), F32)] * (3 * subs)),
        compiler_params=_params(3),
        name="diffattn",
    )(slopes, lam_params, kfeat, proj, proj, proj, subg)


def _mix_kernel(x_ref, ya_ref, yb_ref, g_ref, wa_ref, wb_ref, wo_ref, o_ref):
    d = x_ref.shape[1]
    pa = jnp.dot(ya_ref[...], wa_ref[...], preferred_element_type=F32)
    pb = jnp.dot(yb_ref[...], wb_ref[...], preferred_element_type=F32)
    mixed = g_ref[:, :d].astype(F32) * pa + g_ref[:, d:].astype(F32) * pb
    o_ref[...] = x_ref[...] + jnp.dot(mixed.astype(BF16), wo_ref[...],
                                      preferred_element_type=F32)


def _mix(x2, ya, yb, gate, wa, wb, wo, *, tm):
    t, d = x2.shape
    return pl.pallas_call(
        _mix_kernel,
        grid=(t // tm,),
        in_specs=[
            pl.BlockSpec((tm, d), lambda i: (i, 0)),
            pl.BlockSpec((tm, ya.shape[1]), lambda i: (i, 0)),
            pl.BlockSpec((tm, yb.shape[1]), lambda i: (i, 0)),
            pl.BlockSpec((tm, gate.shape[1]), lambda i: (i, 0)),
            _resident(wa.shape),
            _resident(wb.shape),
            _resident(wo.shape),
        ],
        out_specs=pl.BlockSpec((tm, d), lambda i: (i, 0)),
        out_shape=jax.ShapeDtypeStruct((t, d), F32),
        compiler_params=_params(1),
        name="mix",
    )(x2, ya, yb, gate, wa, wb, wo)


FF_CHUNK = 256


def _ffn_kernel(x_ref, xp_ref, xn_ref, p_ref, gffn_ref, wup_ref, cw_ref, cb_ref, wdn_ref,
                gple_ref, wpg_ref, wpp_ref, o_ref, acc_sc, uv0_sc, uv1_sc, ug0_sc, ug1_sc,
                a0_sc, a1_sc, h_sc, *, tm, seq, d_ff):
    uv_sc = (uv0_sc, uv1_sc)
    ug_sc = (ug0_sc, ug1_sc)
    a_sc = (a0_sc, a1_sc)
    i = pl.program_id(0)
    tiles_per_seq = seq // tm
    pos = i % tiles_per_seq
    keep_prev = (pos != 0).astype(F32)
    keep_next = (pos != tiles_per_seq - 1).astype(F32)
    x = x_ref[...]
    g = gffn_ref[...]

    def norm(v):
        return v * _rms_scale(v) * g

    h_sc[...] = jnp.concatenate(
        [norm(xp_ref[...]) * keep_prev, norm(x), norm(xn_ref[...]) * keep_next],
        axis=0).astype(BF16)
    acc_sc[...] = jnp.zeros(acc_sc.shape, F32)
    n_chunks = d_ff // FF_CHUNK

    def conv(u_ref, c):
        w = cw_ref[c]
        return (w[0:1] * u_ref[HALO - 1:HALO - 1 + tm, :] + w[1:2] * u_ref[HALO:HALO + tm, :]
                + w[2:3] * u_ref[HALO + 1:HALO + 1 + tm, :] + cb_ref[c])

    def up(c, slot):
        uv_sc[slot][...] = jnp.dot(h_sc[...], wup_ref[c], preferred_element_type=F32)
        ug_sc[slot][...] = jnp.dot(h_sc[...], wup_ref[n_chunks + c], preferred_element_type=F32)

    def act(c, slot):
        uv = conv(uv_sc[slot], c)
        ug = conv(ug_sc[slot], n_chunks + c)
        a_sc[slot][...] = (jax.nn.gelu(ug, approximate=True) * uv).astype(BF16)

    def down(c, slot):
        rows = pl.ds(pl.multiple_of(c * FF_CHUNK, FF_CHUNK), FF_CHUNK)
        acc_sc[...] += jnp.dot(a_sc[slot][...], wdn_ref[rows, :], preferred_element_type=F32)

    up(0, 0)
    up(1, 1)
    act(0, 0)
    for c in range(n_chunks):
        if c + 2 < n_chunks:
            up(c + 2, c % 2)
        if c + 1 < n_chunks:
            act(c + 1, (c + 1) % 2)
        down(c, c % 2)

    x2 = x + acc_sc[...]
    h3 = (x2 * _rms_scale(x2) * gple_ref[...]).astype(BF16)
    pg = jax.nn.sigmoid(jnp.dot(h3, wpg_ref[...], preferred_element_type=F32))
    pp = jnp.dot(p_ref[...].astype(BF16), wpp_ref[...], preferred_element_type=F32)
    o_ref[...] = x2 + pg * pp


def _ffn(x1, p2, gffn, wup, cw, cb, wdn, gple, wpg, wpp, *, seq, tm):
    t, d = x1.shape
    d_ff = wdn.shape[0]
    assert seq % tm == 0 and tm % HALO == 0 and d_ff % FF_CHUNK == 0
    hb = tm // HALO
    last = t // HALO - 1
    kern = functools.partial(_ffn_kernel, tm=tm, seq=seq, d_ff=d_ff)
    return pl.pallas_call(
        kern,
        grid=(t // tm,),
        in_specs=[
            pl.BlockSpec((tm, d), lambda i: (i, 0)),
            pl.BlockSpec((HALO, d), lambda i: (jnp.maximum(i * hb - 1, 0), 0)),
            pl.BlockSpec((HALO, d), lambda i: (jnp.minimum((i + 1) * hb, last), 0)),
            pl.BlockSpec((tm, p2.shape[1]), lambda i: (i, 0)),
            _resident(gffn.shape),
            _resident(wup.shape),
            _resident(cw.shape),
            _resident(cb.shape),
            _resident(wdn.shape),
            _resident(gple.shape),
            _resident(wpg.shape),
            _resident(wpp.shape),
        ],
        out_specs=pl.BlockSpec((tm, d), lambda i: (i, 0)),
        out_shape=jax.ShapeDtypeStruct((t, d), F32),
        scratch_shapes=[
            pltpu.VMEM((tm, d), F32),
        ] + [pltpu.VMEM((tm + 2 * HALO, FF_CHUNK), F32)] * 4
          + [pltpu.VMEM((tm, FF_CHUNK), BF16)] * 2
          + [pltpu.VMEM((tm + 2 * HALO, d), BF16)],
        compiler_params=_params(1),
        name="ffn",
    )(x1, x1, x1, p2, gffn, wup, cw, cb, wdn, gple, wpg, wpp)


def kernel(x, p, norm_mix_g, w_in, qn_a_q, qn_a_k, rpb, qn_b_q, qn_b_k, lam_q1, lam_k1, lam_q2, lam_k2, subln_g, w_proj_a, w_proj_b, w_gate, b_gate, w_out, norm_ffn_g, w_up, conv_w, conv_b, w_down, norm_ple_g, w_ple_gate, w_ple_proj):
    batch, seq, d = x.shape
    depth = p.shape[0]
    assert depth == 1
    t = batch * seq
    width_a = w_proj_a.shape[1]
    width_b = w_proj_b.shape[1]
    n_heads_a = width_a // HEAD_DIM
    n_heads_b = width_b // (2 * HEAD_DIM)
    d_in = 3 * width_a + 3 * width_b
    assert width_a == CHUNK and width_b == CHUNK

    scale = 1.0 / math.sqrt(HEAD_DIM)
    ones = jnp.ones((CHUNK,), F32)
    gain = jnp.concatenate([
        jnp.tile(qn_a_q[0], n_heads_a) * scale, jnp.tile(qn_a_k[0], n_heads_a), ones,
        jnp.tile(qn_b_q[0], 2 * n_heads_b) * (scale * LOG2E), jnp.tile(qn_b_k[0], 2 * n_heads_b), ones,
    ])[None, :]
    normed = (True, True, False, True, True, False)
    bd = jnp.asarray(np.kron(np.eye(SLAB // HEAD_DIM), np.full((HEAD_DIM, HEAD_DIM), 1.0 / HEAD_DIM)),
                     BF16)
    w_cat = jnp.concatenate([w_in[0], w_gate[0]], axis=1).astype(BF16)

    x2 = x.reshape(t, d)
    proj, gate = _inproj(x2, norm_mix_g, w_cat, b_gate, gain, bd,
                         d_in=d_in, normed=normed, tm=512)

    e_tab = _bias_table(rpb[0])
    blk = CHUNK // LANES
    ya = _natten(proj, e_tab, batch=batch, seq=seq, n_pairs=n_heads_a // 2,
                 q_blk=0, k_blk=blk, v_blk=2 * blk, groups=4)

    slopes = jnp.asarray(LOG2E * 2.0 ** (-8.0 * (np.arange(n_heads_b) + 1.0) / n_heads_b), F32)
    lam_params = jnp.concatenate([lam_q1, lam_k1, lam_q2, lam_k2], axis=0)
    yb = _diffattn(proj, slopes, lam_params, subln_g, batch=batch, seq=seq, n_heads=n_heads_b,
                   q_blk=3 * blk, k_blk=4 * blk, v_blk=5 * blk, tq=256, tk=256, subs=2)

    x1 = _mix(x2, ya, yb, gate, w_proj_a[0].astype(BF16), w_proj_b[0].astype(BF16),
              w_out[0].astype(BF16), tm=512)

    n_col = w_up.shape[2] // FF_CHUNK
    wup3 = w_up[0].astype(BF16).reshape(d, n_col, FF_CHUNK).transpose(1, 0, 2)
    cw3 = conv_w[0].reshape(CONV_W, n_col, FF_CHUNK).transpose(1, 0, 2)
    cb3 = conv_b.reshape(n_col, 1, FF_CHUNK)
    out = _ffn(x1, p[0].reshape(t, -1), norm_ffn_g, wup3, cw3, cb3,
               w_down[0].astype(BF16), norm_ple_g, w_ple_gate[0].astype(BF16),
               w_ple_proj[0].astype(BF16), seq=seq, tm=512)
    return out.reshape(batch, seq, d)
```

```python
import functools
import math

import jax
import jax.numpy as jnp
import numpy as np
from jax import lax
from jax.experimental import pallas as pl
from jax.experimental.pallas import tpu as pltpu

F32 = jnp.float32
BF16 = jnp.bfloat16

HEAD_DIM = 64
GRID_W = 64
NA_KH = 8
NA_KW = 16
CONV_W = 3
EPS = 1e-6
LAM_INIT = 0.8 - 0.6 * math.exp(-0.3 * 0)
NEG_BIG = -1e30
LOG2E = math.log2(math.e)

LANES = 128
HALO = 8
VMEM_LIMIT = 56 * 1024 * 1024


def _params(n_axes, vmem=VMEM_LIMIT):
    return pltpu.CompilerParams(
        dimension_semantics=("arbitrary",) * n_axes, vmem_limit_bytes=vmem)


def _resident(shape):
    nd = len(shape)
    return pl.BlockSpec(shape, lambda *_: (0,) * nd, pipeline_mode=pl.Buffered(1))


def _rms_scale(x):
    return lax.rsqrt(jnp.mean(x * x, axis=-1, keepdims=True) + EPS)


ROW_GROUP = 4
KEY_ROWS = 12
_GROUP_CLASSES = (
    (0, lambda i: 0),
    (-(NA_KH // 2), lambda i: i - NA_KH // 2),
    (ROW_GROUP - KEY_ROWS, lambda i: ROW_GROUP - NA_KH),
)


def _bias_table_kernel(rpb_ref, e_ref):
    h = pl.program_id(0)
    n_dr = 2 * NA_KH - 1
    n_dc = 2 * NA_KW - 1
    cq = lax.broadcasted_iota(jnp.int32, (GRID_W, LANES), 0)
    lane = lax.broadcasted_iota(jnp.int32, (GRID_W, LANES), 1)
    ck = lane % GRID_W
    upper = lane >= GRID_W
    dc = jnp.clip(ck - cq, -(NA_KW - 1), NA_KW - 1) + (NA_KW - 1)
    cs = jnp.clip(cq - NA_KW // 2, 0, GRID_W - NA_KW)
    inside = (ck >= cs) & (ck < cs + NA_KW)
    neg = jnp.full((GRID_W, LANES), NEG_BIG, F32)
    tiles = []
    for dr in range(n_dr):
        acc = neg
        for d in range(n_dc):
            acc = jnp.where(inside & (dc == d), rpb_ref[h * n_dr + dr, d], acc)
        tiles.append(acc)

    def half(ws_off, rs_off, i, j):
        row = ws_off + j
        if rs_off <= row < rs_off + NA_KH:
            return tiles[row - i + NA_KH - 1]
        return neg

    for cls, (ws_off, rs_fn) in enumerate(_GROUP_CLASSES):
        for i in range(ROW_GROUP):
            for j in range(0, KEY_ROWS, 2):
                lo = half(ws_off, rs_fn(i), i, j)
                hi = half(ws_off, rs_fn(i), i, j + 1)
                e_ref[0, cls, i * GRID_W:(i + 1) * GRID_W, j * GRID_W:(j + 2) * GRID_W] = (
                    jnp.where(upper, hi, lo))


def _bias_table(rpb):
    n_heads = rpb.shape[0]
    rpb2 = rpb.reshape(n_heads * (2 * NA_KH - 1), 2 * NA_KW - 1)
    shape = (n_heads, len(_GROUP_CLASSES), ROW_GROUP * GRID_W, KEY_ROWS * GRID_W)
    return pl.pallas_call(
        _bias_table_kernel,
        grid=(n_heads,),
        in_specs=[pl.BlockSpec(memory_space=pltpu.SMEM)],
        out_specs=pl.BlockSpec((1,) + shape[1:], lambda h: (h, 0, 0, 0)),
        out_shape=jax.ShapeDtypeStruct(shape, F32),
        compiler_params=_params(1),
        name="bias_table",
    )(rpb2)


CHUNK = 512
SLAB = 256


def _inproj_kernel(x_ref, gmix_ref, w_ref, wvt_ref, bgate_ref, gain_ref, bd_ref,
                   proj_ref, gate_ref, vt_ref, *, d_in, normed):
    x = x_ref[...]
    h = (x * _rms_scale(x) * gmix_ref[...]).astype(BF16)
    vt_ref[0] = lax.dot_general(wvt_ref[...], h, (((1,), (1,)), ((), ())),
                                preferred_element_type=F32).astype(BF16)
    for c in range(d_in // CHUNK):
        p = jnp.dot(h, w_ref[:, c * CHUNK:(c + 1) * CHUNK], preferred_element_type=F32)
        if normed[c]:
            for s in range(CHUNK // SLAB):
                col = slice(c * CHUNK + s * SLAB, c * CHUNK + (s + 1) * SLAB)
                ps = p[:, s * SLAB:(s + 1) * SLAB]
                ms = jnp.dot((ps * ps).astype(BF16), bd_ref[...], preferred_element_type=F32)
                proj_ref[:, col] = (ps * lax.rsqrt(ms + EPS) * gain_ref[:, col]).astype(BF16)
        else:
            proj_ref[:, c * CHUNK:(c + 1) * CHUNK] = p.astype(BF16)
    d_gate = gate_ref.shape[1]
    for c in range(d_gate // CHUNK):
        col = slice(c * CHUNK, (c + 1) * CHUNK)
        z = jnp.dot(h, w_ref[:, d_in + c * CHUNK:d_in + (c + 1) * CHUNK],
                    preferred_element_type=F32) + bgate_ref[:, col]
        gate_ref[:, col] = jax.nn.sigmoid(z).astype(BF16)


def _inproj(x2, gmix, w_cat, wvt, b_gate, gain, bd, *, d_in, normed, tm):
    t, d = x2.shape
    d_gate = w_cat.shape[1] - d_in
    d_v = wvt.shape[0]
    kern = functools.partial(_inproj_kernel, d_in=d_in, normed=normed)
    return pl.pallas_call(
        kern,
        grid=(t // tm,),
        in_specs=[
            pl.BlockSpec((tm, d), lambda i: (i, 0)),
            _resident(gmix.shape),
            _resident(w_cat.shape),
            _resident(wvt.shape),
            _resident(b_gate.shape),
            _resident(gain.shape),
            _resident(bd.shape),
        ],
        out_specs=[
            pl.BlockSpec((tm, d_in), lambda i: (i, 0)),
            pl.BlockSpec((tm, d_gate), lambda i: (i, 0)),
            pl.BlockSpec((1, d_v, tm), lambda i: (i, 0, 0)),
        ],
        out_shape=[
            jax.ShapeDtypeStruct((t, d_in), BF16),
            jax.ShapeDtypeStruct((t, d_gate), BF16),
            jax.ShapeDtypeStruct((t // tm, d_v, tm), BF16),
        ],
        compiler_params=_params(1),
        name="inproj",
    )(x2, gmix, w_cat, wvt, b_gate, gain, bd)


def _natten_kernel(q_ref, k_ref, v_ref, e_ref, o_ref, *, groups, n_rows):
    gblk = pl.program_id(2)
    gq = ROW_GROUP * GRID_W
    win = KEY_ROWS * GRID_W
    lane = lax.broadcasted_iota(jnp.int32, (gq, LANES), 1)
    first = lane < HEAD_DIM
    for gi in range(groups):
        r0 = (gblk * groups + gi) * ROW_GROUP
        ws = jnp.clip(r0 - NA_KH // 2, 0, n_rows - KEY_ROWS)
        cls = jnp.where(r0 == 0, 0, jnp.where(r0 == n_rows - ROW_GROUP, 2, 1))
        q = q_ref[gi * gq:(gi + 1) * gq, :]
        k0 = pl.multiple_of(ws * GRID_W, GRID_W)
        kw = k_ref[pl.ds(k0, win), :]
        vw = v_ref[pl.ds(k0, win), :]
        outs = []
        for hh in range(2):
            qm = jnp.where(first if hh == 0 else jnp.logical_not(first), q, jnp.zeros_like(q))
            s = lax.dot_general(qm, kw, (((1,), (1,)), ((), ())), preferred_element_type=F32)
            s = s + e_ref[hh, cls]
            m = jnp.max(s, axis=-1, keepdims=True)
            p = jnp.exp(s - m)
            l = jnp.sum(p, axis=-1, keepdims=True)
            outs.append(jnp.dot(p.astype(BF16), vw, preferred_element_type=F32) / l)
        o_ref[gi * gq:(gi + 1) * gq, :] = jnp.where(first, outs[0], outs[1]).astype(BF16)


def _natten(proj, e_tab, *, batch, seq, n_pairs, q_blk, k_blk, v_blk, groups):
    n_rows = seq // GRID_W
    rb = groups * ROW_GROUP
    assert n_rows >= KEY_ROWS and n_rows % rb == 0
    steps = n_rows // rb
    kern = functools.partial(_natten_kernel, groups=groups, n_rows=n_rows)
    return pl.pallas_call(
        kern,
        grid=(n_pairs, batch, steps),
        in_specs=[
            pl.BlockSpec((rb * GRID_W, LANES), lambda hp, b, r: (b * steps + r, q_blk + hp)),
            pl.BlockSpec((seq, LANES), lambda hp, b, r: (b, k_blk + hp)),
            pl.BlockSpec((seq, LANES), lambda hp, b, r: (b, v_blk + hp)),
            pl.BlockSpec((2,) + e_tab.shape[1:], lambda hp, b, r: (hp, 0, 0, 0)),
        ],
        out_specs=pl.BlockSpec((rb * GRID_W, LANES), lambda hp, b, r: (b * steps + r, hp)),
        out_shape=jax.ShapeDtypeStruct((batch * seq, n_pairs * LANES), BF16),
        compiler_params=_params(3),
        name="natten",
    )(proj, proj, proj, e_tab)


POS_RADIX = 64
N_POS_FEATS = 8
MAX_SHIFT_GAP = 90.0
BOUND_SLACK = 2.0 ** -6


def _split_bf16_np(x):
    hi = np.asarray(x, np.float32).astype(BF16).astype(np.float32)
    return hi, np.float32(x - hi)


def _pos_feats(seq, slopes):
    pos = np.arange(seq)
    hi = (POS_RADIX * (pos // POS_RADIX)).astype(np.float32)
    lo = (pos % POS_RADIX).astype(np.float32)
    one = np.ones(seq, np.float32)
    zero = np.zeros(seq, np.float32)
    qs, ks = [], []
    for slope in np.asarray(slopes, np.float32):
        s_hi, s_lo = (float(v) for v in _split_bf16_np(np.float32(slope)))
        qs.append(np.stack([hi, hi, lo, lo, s_hi * one, s_lo * one, s_hi * one, s_lo * one,
                            zero, zero], axis=1))
        ks.append(np.stack([s_hi * one, s_lo * one, s_hi * one, s_lo * one, -hi, -hi, -lo, -lo,
                            one, one], axis=1))

    def table(feats):
        half = np.zeros((len(feats), seq, HEAD_DIM), np.float32)
        half[:, :, :feats[0].shape[1]] = np.stack(feats)
        return jnp.asarray(np.concatenate([half, half], axis=2), BF16)

    return table(qs), table(ks)


def _split_bf16(x):
    hi = x.astype(BF16).astype(F32)
    return hi, x - hi


def _shift_feats(shift, tq):
    lane = lax.broadcasted_iota(jnp.int32, (tq, LANES), 1) % HEAD_DIM
    hi, lo = _split_bf16(-shift)
    return jnp.where(lane == N_POS_FEATS, hi, jnp.where(lane == N_POS_FEATS + 1, lo, 0.0))


def _half_sums(x, swap_ref):
    return jnp.dot(x, swap_ref[...], preferred_element_type=F32)


def _score_bound(qn2, kmax2):
    return jnp.sqrt(qn2 * kmax2) * (1.0 + BOUND_SLACK) + BOUND_SLACK


def _guard_kernel(swap_ref, q_ref, k_ref, kmax_ref, flag_ref, *, tq):
    q = q_ref[...]
    k = k_ref[...]
    qn2 = _half_sums(q * q, swap_ref)
    kn2 = _half_sums(k * k, swap_ref)
    self_score = _half_sums(q * k, swap_ref)
    kmax2 = jnp.max(kn2, axis=0, keepdims=True)
    kmax_ref[0] = jnp.broadcast_to(kmax2, kmax_ref.shape[1:])
    gap = _score_bound(qn2, kmax2) - self_score
    n_tiles = gap.shape[0] // tq
    worst = jnp.max(jnp.max(gap.reshape(n_tiles, tq, LANES), axis=1), axis=-1, keepdims=True)
    safe = worst <= MAX_SHIFT_GAP
    flag_ref[0] = jnp.broadcast_to(jnp.where(safe, 0, 1), flag_ref.shape[1:]).astype(jnp.int32)


def _guard(proj, swap, *, batch, seq, n_heads, q_blk, k_blk, tq):
    n_tiles = seq // tq
    kern = functools.partial(_guard_kernel, tq=tq)
    return pl.pallas_call(
        kern,
        grid=(batch, n_heads),
        in_specs=[
            _resident(swap.shape),
            pl.BlockSpec((seq, LANES), lambda b, h: (b, q_blk + h)),
            pl.BlockSpec((seq, LANES), lambda b, h: (b, k_blk + h)),
        ],
        out_specs=[
            pl.BlockSpec((1, HALO, LANES), lambda b, h: (b * n_heads + h, 0, 0)),
            pl.BlockSpec((1, n_tiles, LANES), lambda b, h: (b * n_heads + h, 0, 0)),
        ],
        out_shape=[
            jax.ShapeDtypeStruct((batch * n_heads, HALO, LANES), F32),
            jax.ShapeDtypeStruct((batch * n_heads, n_tiles, LANES), jnp.int32),
        ],
        compiler_params=_params(2),
        name="attn_guard",
    )(swap, proj, proj)


def _diffattn_kernel(slopes_ref, flags_ref, lamp_ref, dist_ref, qf_ref, kf_ref, swap_ref,
                     kmax_ref, q_ref, k_ref, vt_ref, subg_ref, o_ref, m_sc, l_sc, acc_sc,
                     *, tq, tk, seq, n_heads):
    b = pl.program_id(0)
    h = pl.program_id(1)
    qi = pl.program_id(2)
    q = q_ref[...]
    lane = lax.broadcasted_iota(jnp.int32, (tq, LANES), 1)
    first = lane < HEAD_DIM
    n_chunks = seq // tk
    n_slabs = tk // LANES
    posf = qf_ref[0].astype(F32)
    diag_bias = dist_ref[...] * (-slopes_ref[h])

    def operands(feats):
        f = feats.astype(BF16)
        return jnp.where(first, q, f), jnp.where(first, f, q)

    def for_chunks(shiftf, consume, unroll, keys_major):
        q_diag = operands(shiftf)
        q_after = operands(posf + shiftf)
        q_before = operands(shiftf - posf)

        def chunk(c, qs, bias):
            k0 = pl.multiple_of(c * tk, tk)
            kc = k_ref[pl.ds(k0, tk), :]
            kf = kf_ref[0, pl.ds(k0, tk), :]
            ks = (jnp.where(first, kc, kf), jnp.where(first, kf, kc))
            ss = []
            for mi in range(2):
                lhs, rhs = (ks[mi], qs[mi]) if keys_major else (qs[mi], ks[mi])
                s = lax.dot_general(lhs, rhs, (((1,), (1,)), ((), ())),
                                    preferred_element_type=F32)
                ss.append(s if bias is None else s + bias)
            consume(c, ss)

        chunk(qi, q_diag, diag_bias)

        def body(d, carry):
            c = lax.rem(qi + d, n_chunks)
            after = c > qi
            qs = tuple(jnp.where(after, q_after[mi], q_before[mi]) for mi in range(2))
            chunk(c, qs, None)
            return carry

        lax.fori_loop(1, n_chunks, body, 0, unroll=unroll)

    def attend(shift, unroll):
        l_sc[...] = jnp.zeros(l_sc.shape, F32)
        acc_sc[...] = jnp.zeros(acc_sc.shape, F32)

        def consume(c, ss):
            vt = vt_ref[c]
            for mi in range(2):
                p = jnp.exp2(ss[mi])
                l_sc[mi] += jnp.sum(p.reshape(tk // HALO, HALO, tq), axis=0)
                acc_sc[mi] += jnp.dot(vt, p.astype(BF16), preferred_element_type=F32)

        for_chunks(_shift_feats(shift, tq), consume, unroll, True)

    needs_max = flags_ref[b * n_heads + h, qi] != 0

    @pl.when(jnp.logical_not(needs_max))
    def _():
        attend(_score_bound(_half_sums(q * q, swap_ref), kmax_ref[0, 0:1, :]), True)

    @pl.when(needs_max)
    def _():
        m_sc[...] = jnp.full(m_sc.shape, NEG_BIG, F32)

        def consume(c, ss):
            for mi in range(2):
                m = m_sc[mi]
                for j in range(n_slabs):
                    m = jnp.maximum(m, ss[mi][:, j * LANES:(j + 1) * LANES])
                m_sc[mi] = m

        for_chunks(jnp.zeros((tq, LANES), F32), consume, False, False)
        rowmax = [jnp.max(m_sc[mi], axis=-1, keepdims=True) for mi in range(2)]
        attend(jnp.where(first, rowmax[1], rowmax[0]), False)

    lp = lamp_ref[...]
    lam = (jnp.exp(jnp.sum(lp[0:1] * lp[1:2], axis=-1, keepdims=True))
           - jnp.exp(jnp.sum(lp[2:3] * lp[3:4], axis=-1, keepdims=True)) + LAM_INIT)
    ys = [acc_sc[mi] / jnp.sum(l_sc[mi], axis=0, keepdims=True) for mi in range(2)]
    yt = ys[0] - lam * ys[1]
    yt = yt * lax.rsqrt(jnp.mean(yt * yt, axis=0, keepdims=True) + EPS)
    o_ref[...] = (yt.T * subg_ref[...] * (1.0 - LAM_INIT)).astype(BF16)


def _diffattn(proj, vt, slopes, lam_params, subg, *, batch, seq, n_heads, q_blk, k_blk,
              tq, tk):
    steps = seq // tq
    assert tq == tk and seq % tk == 0 and seq <= POS_RADIX * 256
    assert vt.shape == (batch * seq // tk, n_heads * LANES, tk)
    qfeat, kfeat = _pos_feats(seq, slopes)
    idx = np.arange(tq)
    dist = jnp.asarray(np.abs(idx[:, None] - idx[None, :]), F32)
    half = np.arange(LANES) // HEAD_DIM
    swap = jnp.asarray(half[:, None] != half[None, :], BF16)
    kmax2, flags = _guard(proj, swap, batch=batch, seq=seq, n_heads=n_heads,
                          q_blk=q_blk, k_blk=k_blk, tq=tq)
    kern = functools.partial(_diffattn_kernel, tq=tq, tk=tk, seq=seq, n_heads=n_heads)
    return pl.pallas_call(
        kern,
        grid=(batch, n_heads, steps),
        in_specs=[
            pl.BlockSpec(memory_space=pltpu.SMEM),
            pl.BlockSpec(memory_space=pltpu.SMEM),
            pl.BlockSpec(lam_params.shape, lambda b, h, i: (0, 0)),
            _resident(dist.shape),
            pl.BlockSpec((1, tq, LANES), lambda b, h, i: (h, i, 0)),
            pl.BlockSpec((1, seq, LANES), lambda b, h, i: (h, 0, 0)),
            _resident(swap.shape),
            pl.BlockSpec((1, HALO, LANES), lambda b, h, i: (b * n_heads + h, 0, 0)),
            pl.BlockSpec((tq, LANES), lambda b, h, i: (b * steps + i, q_blk + h)),
            pl.BlockSpec((seq, LANES), lambda b, h, i: (b, k_blk + h)),
            pl.BlockSpec((seq // tk, LANES, tk), lambda b, h, i: (b, h, 0)),
            pl.BlockSpec(subg.shape, lambda b, h, i: (0, 0)),
        ],
        out_specs=pl.BlockSpec((tq, LANES), lambda b, h, i: (b * steps + i, h)),
        out_shape=jax.ShapeDtypeStruct((batch * seq, n_heads * LANES), BF16),
        scratch_shapes=[
            pltpu.VMEM((2, tq, LANES), F32),
            pltpu.VMEM((2, HALO, tq), F32),
            pltpu.VMEM((2, LANES, tq), F32),
        ],
        compiler_params=_params(3),
        name="diffattn",
    )(jnp.asarray(slopes, F32), flags[:, :, 0], lam_params, dist, qfeat, kfeat, swap, kmax2,
      proj, proj, vt, subg)


def _mix_kernel(x_ref, ya_ref, yb_ref, g_ref, wa_ref, wb_ref, wo_ref, o_ref):
    d = x_ref.shape[1]
    pa = jnp.dot(ya_ref[...], wa_ref[...], preferred_element_type=F32)
    pb = jnp.dot(yb_ref[...], wb_ref[...], preferred_element_type=F32)
    mixed = g_ref[:, :d].astype(F32) * pa + g_ref[:, d:].astype(F32) * pb
    o_ref[...] = x_ref[...] + jnp.dot(mixed.astype(BF16), wo_ref[...],
                                      preferred_element_type=F32)


def _mix(x2, ya, yb, gate, wa, wb, wo, *, tm):
    t, d = x2.shape
    return pl.pallas_call(
        _mix_kernel,
        grid=(t // tm,),
        in_specs=[
            pl.BlockSpec((tm, d), lambda i: (i, 0)),
            pl.BlockSpec((tm, ya.shape[1]), lambda i: (i, 0)),
            pl.BlockSpec((tm, yb.shape[1]), lambda i: (i, 0)),
            pl.BlockSpec((tm, gate.shape[1]), lambda i: (i, 0)),
            _resident(wa.shape),
            _resident(wb.shape),
            _resident(wo.shape),
        ],
        out_specs=pl.BlockSpec((tm, d), lambda i: (i, 0)),
        out_shape=jax.ShapeDtypeStruct((t, d), F32),
        compiler_params=_params(1),
        name="mix",
    )(x2, ya, yb, gate, wa, wb, wo)


FF_CHUNK = 256


def _ffn_kernel(x_ref, xp_ref, xn_ref, p_ref, gffn_ref, wup_ref, cw_ref, cb_ref, wdn_ref,
                gple_ref, wpg_ref, wpp_ref, o_ref, acc_sc, uv0_sc, uv1_sc, ug0_sc, ug1_sc,
                a0_sc, a1_sc, h_sc, *, tm, seq, d_ff):
    uv_sc = (uv0_sc, uv1_sc)
    ug_sc = (ug0_sc, ug1_sc)
    a_sc = (a0_sc, a1_sc)
    i = pl.program_id(0)
    tiles_per_seq = seq // tm
    pos = i % tiles_per_seq
    keep_prev = (pos != 0).astype(F32)
    keep_next = (pos != tiles_per_seq - 1).astype(F32)
    x = x_ref[...]
    g = gffn_ref[...]

    def norm(v):
        return v * _rms_scale(v) * g

    h_sc[...] = jnp.concatenate(
        [norm(xp_ref[...]) * keep_prev, norm(x), norm(xn_ref[...]) * keep_next],
        axis=0).astype(BF16)
    acc_sc[...] = jnp.zeros(acc_sc.shape, F32)
    n_chunks = d_ff // FF_CHUNK

    def conv(u_ref, c):
        w = cw_ref[c]
        return (w[0:1] * u_ref[HALO - 1:HALO - 1 + tm, :] + w[1:2] * u_ref[HALO:HALO + tm, :]
                + w[2:3] * u_ref[HALO + 1:HALO + 1 + tm, :] + cb_ref[c])

    def up(c, slot):
        uv_sc[slot][...] = jnp.dot(h_sc[...], wup_ref[c], preferred_element_type=F32)
        ug_sc[slot][...] = jnp.dot(h_sc[...], wup_ref[n_chunks + c], preferred_element_type=F32)

    def act(c, slot):
        uv = conv(uv_sc[slot], c)
        ug = conv(ug_sc[slot], n_chunks + c)
        a_sc[slot][...] = (jax.nn.gelu(ug, approximate=True) * uv).astype(BF16)

    def down(c, slot):
        rows = pl.ds(pl.multiple_of(c * FF_CHUNK, FF_CHUNK), FF_CHUNK)
        acc_sc[...] += jnp.dot(a_sc[slot][...], wdn_ref[rows, :], preferred_element_type=F32)

    up(0, 0)
    up(1, 1)
    act(0, 0)
    for c in range(n_chunks):
        if c + 2 < n_chunks:
            up(c + 2, c % 2)
        if c + 1 < n_chunks:
            act(c + 1, (c + 1) % 2)
        down(c, c % 2)

    x2 = x + acc_sc[...]
    h3 = (x2 * _rms_scale(x2) * gple_ref[...]).astype(BF16)
    pg = jax.nn.sigmoid(jnp.dot(h3, wpg_ref[...], preferred_element_type=F32))
    pp = jnp.dot(p_ref[...].astype(BF16), wpp_ref[...], preferred_element_type=F32)
    o_ref[...] = x2 + pg * pp


def _ffn(x1, p2, gffn, wup, cw, cb, wdn, gple, wpg, wpp, *, seq, tm):
    t, d = x1.shape
    d_ff = wdn.shape[0]
    assert seq % tm == 0 and tm % HALO == 0 and d_ff % FF_CHUNK == 0
    hb = tm // HALO
    last = t // HALO - 1
    kern = functools.partial(_ffn_kernel, tm=tm, seq=seq, d_ff=d_ff)
    return pl.pallas_call(
        kern,
        grid=(t // tm,),
        in_specs=[
            pl.BlockSpec((tm, d), lambda i: (i, 0)),
            pl.BlockSpec((HALO, d), lambda i: (jnp.maximum(i * hb - 1, 0), 0)),
            pl.BlockSpec((HALO, d), lambda i: (jnp.minimum((i + 1) * hb, last), 0)),
            pl.BlockSpec((tm, p2.shape[1]), lambda i: (i, 0)),
            _resident(gffn.shape),
            _resident(wup.shape),
            _resident(cw.shape),
            _resident(cb.shape),
            _resident(wdn.shape),
            _resident(gple.shape),
            _resident(wpg.shape),
            _resident(wpp.shape),
        ],
        out_specs=pl.BlockSpec((tm, d), lambda i: (i, 0)),
        out_shape=jax.ShapeDtypeStruct((t, d), F32),
        scratch_shapes=[
            pltpu.VMEM((tm, d), F32),
        ] + [pltpu.VMEM((tm + 2 * HALO, FF_CHUNK), F32)] * 4
          + [pltpu.VMEM((tm, FF_CHUNK), BF16)] * 2
          + [pltpu.VMEM((tm + 2 * HALO, d), BF16)],
        compiler_params=_params(1),
        name="ffn",
    )(x1, x1, x1, p2, gffn, wup, cw, cb, wdn, gple, wpg, wpp)


def kernel(x, p, norm_mix_g, w_in, qn_a_q, qn_a_k, rpb, qn_b_q, qn_b_k, lam_q1, lam_k1, lam_q2, lam_k2, subln_g, w_proj_a, w_proj_b, w_gate, b_gate, w_out, norm_ffn_g, w_up, conv_w, conv_b, w_down, norm_ple_g, w_ple_gate, w_ple_proj):
    batch, seq, d = x.shape
    depth = p.shape[0]
    assert depth == 1
    t = batch * seq
    width_a = w_proj_a.shape[1]
    width_b = w_proj_b.shape[1]
    n_heads_a = width_a // HEAD_DIM
    n_heads_b = width_b // (2 * HEAD_DIM)
    d_in = 3 * width_a + 2 * width_b
    assert width_a == CHUNK and width_b == CHUNK

    scale = 1.0 / math.sqrt(HEAD_DIM)
    ones = jnp.ones((CHUNK,), F32)
    gain = jnp.concatenate([
        jnp.tile(qn_a_q[0], n_heads_a) * scale, jnp.tile(qn_a_k[0], n_heads_a), ones,
        jnp.tile(qn_b_q[0], 2 * n_heads_b) * (scale * LOG2E), jnp.tile(qn_b_k[0], 2 * n_heads_b),
    ])[None, :]
    normed = (True, True, False, True, True)
    bd = jnp.asarray(np.kron(np.eye(SLAB // HEAD_DIM), np.full((HEAD_DIM, HEAD_DIM), 1.0 / HEAD_DIM)),
                     BF16)
    w_cat = jnp.concatenate([w_in[0][:, :d_in], w_gate[0]], axis=1).astype(BF16)
    wvt = w_in[0][:, d_in:].T.astype(BF16)

    x2 = x.reshape(t, d)
    proj, gate, vt = _inproj(x2, norm_mix_g, w_cat, wvt, b_gate, gain, bd,
                             d_in=d_in, normed=normed, tm=512)

    e_tab = _bias_table(rpb[0])
    blk = CHUNK // LANES
    ya = _natten(proj, e_tab, batch=batch, seq=seq, n_pairs=n_heads_a // 2,
                 q_blk=0, k_blk=blk, v_blk=2 * blk, groups=4)

    slopes = (LOG2E * 2.0 ** (-8.0 * (np.arange(n_heads_b) + 1.0) / n_heads_b)).astype(np.float32)
    lam_params = jnp.concatenate([lam_q1, lam_k1, lam_q2, lam_k2], axis=0)
    yb = _diffattn(proj, vt, slopes, lam_params, subln_g, batch=batch, seq=seq, n_heads=n_heads_b,
                   q_blk=3 * blk, k_blk=4 * blk, tq=512, tk=512)

    x1 = _mix(x2, ya, yb, gate, w_proj_a[0].astype(BF16), w_proj_b[0].astype(BF16),
              w_out[0].astype(BF16), tm=512)

    n_col = w_up.shape[2] // FF_CHUNK
    wup3 = w_up[0].astype(BF16).reshape(d, n_col, FF_CHUNK).transpose(1, 0, 2)
    cw3 = conv_w[0].reshape(CONV_W, n_col, FF_CHUNK).transpose(1, 0, 2)
    cb3 = conv_b.reshape(n_col, 1, FF_CHUNK)
    out = _ffn(x1, p[0].reshape(t, -1), norm_ffn_g, wup3, cw3, cb3,
               w_down[0].astype(BF16), norm_ple_g, w_ple_gate[0].astype(BF16),
               w_ple_proj[0].astype(BF16), seq=seq, tm=512)
    return out.reshape(batch, seq, d)
```

```python
import functools
import math

import jax
import jax.numpy as jnp
import numpy as np
from jax import lax
from jax.experimental import pallas as pl
from jax.experimental.pallas import tpu as pltpu

F32 = jnp.float32
BF16 = jnp.bfloat16

HEAD_DIM = 64
GRID_W = 64
NA_KH = 8
NA_KW = 16
CONV_W = 3
EPS = 1e-6
LAM_INIT = 0.8 - 0.6 * math.exp(-0.3 * 0)
NEG_BIG = -1e30
LOG2E = math.log2(math.e)

LANES = 128
HALO = 8
VMEM_LIMIT = 56 * 1024 * 1024


def _params(n_axes, vmem=VMEM_LIMIT):
    return pltpu.CompilerParams(
        dimension_semantics=("arbitrary",) * n_axes, vmem_limit_bytes=vmem)


def _resident(shape):
    nd = len(shape)
    return pl.BlockSpec(shape, lambda *_: (0,) * nd, pipeline_mode=pl.Buffered(1))


def _rms_scale(x):
    return lax.rsqrt(jnp.mean(x * x, axis=-1, keepdims=True) + EPS)


ROW_GROUP = 4
KEY_ROWS = 12
_GROUP_CLASSES = (
    (0, lambda i: 0),
    (-(NA_KH // 2), lambda i: i - NA_KH // 2),
    (ROW_GROUP - KEY_ROWS, lambda i: ROW_GROUP - NA_KH),
)


def _bias_table_kernel(rpb_ref, e_ref):
    h = pl.program_id(0)
    n_dr = 2 * NA_KH - 1
    n_dc = 2 * NA_KW - 1
    cq = lax.broadcasted_iota(jnp.int32, (GRID_W, LANES), 0)
    lane = lax.broadcasted_iota(jnp.int32, (GRID_W, LANES), 1)
    ck = lane % GRID_W
    upper = lane >= GRID_W
    dc = jnp.clip(ck - cq, -(NA_KW - 1), NA_KW - 1) + (NA_KW - 1)
    cs = jnp.clip(cq - NA_KW // 2, 0, GRID_W - NA_KW)
    inside = (ck >= cs) & (ck < cs + NA_KW)
    neg = jnp.full((GRID_W, LANES), NEG_BIG, F32)
    tiles = []
    for dr in range(n_dr):
        acc = neg
        for d in range(n_dc):
            acc = jnp.where(inside & (dc == d), rpb_ref[h * n_dr + dr, d] * LOG2E, acc)
        tiles.append(acc)

    def half(ws_off, rs_off, i, j):
        row = ws_off + j
        if rs_off <= row < rs_off + NA_KH:
            return tiles[row - i + NA_KH - 1]
        return neg

    for cls, (ws_off, rs_fn) in enumerate(_GROUP_CLASSES):
        for i in range(ROW_GROUP):
            for j in range(0, KEY_ROWS, 2):
                lo = half(ws_off, rs_fn(i), i, j)
                hi = half(ws_off, rs_fn(i), i, j + 1)
                e_ref[0, cls, i * GRID_W:(i + 1) * GRID_W, j * GRID_W:(j + 2) * GRID_W] = (
                    jnp.where(upper, hi, lo))


def _bias_table(rpb):
    n_heads = rpb.shape[0]
    rpb2 = rpb.reshape(n_heads * (2 * NA_KH - 1), 2 * NA_KW - 1)
    shape = (n_heads, len(_GROUP_CLASSES), ROW_GROUP * GRID_W, KEY_ROWS * GRID_W)
    return pl.pallas_call(
        _bias_table_kernel,
        grid=(n_heads,),
        in_specs=[pl.BlockSpec(memory_space=pltpu.SMEM)],
        out_specs=pl.BlockSpec((1,) + shape[1:], lambda h: (h, 0, 0, 0)),
        out_shape=jax.ShapeDtypeStruct(shape, F32),
        compiler_params=_params(1),
        name="bias_table",
    )(rpb2)


CHUNK = 512
SLAB = 256


def _inproj_kernel(x_ref, gmix_ref, w_ref, wvt_ref, bgate_ref, gain_ref, bd_ref,
                   proj_ref, gate_ref, vt_ref, *, d_in, normed):
    x = x_ref[...]
    h = (x * _rms_scale(x) * gmix_ref[...]).astype(BF16)
    vt_ref[0] = lax.dot_general(wvt_ref[...], h, (((1,), (1,)), ((), ())),
                                preferred_element_type=F32).astype(BF16)
    for c in range(d_in // CHUNK):
        p = jnp.dot(h, w_ref[:, c * CHUNK:(c + 1) * CHUNK], preferred_element_type=F32)
        if normed[c]:
            for s in range(CHUNK // SLAB):
                col = slice(c * CHUNK + s * SLAB, c * CHUNK + (s + 1) * SLAB)
                ps = p[:, s * SLAB:(s + 1) * SLAB]
                ms = jnp.dot((ps * ps).astype(BF16), bd_ref[...], preferred_element_type=F32)
                proj_ref[:, col] = (ps * lax.rsqrt(ms + EPS) * gain_ref[:, col]).astype(BF16)
        else:
            proj_ref[:, c * CHUNK:(c + 1) * CHUNK] = p.astype(BF16)
    d_gate = gate_ref.shape[1]
    for c in range(d_gate // CHUNK):
        col = slice(c * CHUNK, (c + 1) * CHUNK)
        z = jnp.dot(h, w_ref[:, d_in + c * CHUNK:d_in + (c + 1) * CHUNK],
                    preferred_element_type=F32) + bgate_ref[:, col]
        gate_ref[:, col] = jax.nn.sigmoid(z).astype(BF16)


def _inproj(x2, gmix, w_cat, wvt, b_gate, gain, bd, *, d_in, normed, tm):
    t, d = x2.shape
    d_gate = w_cat.shape[1] - d_in
    d_v = wvt.shape[0]
    kern = functools.partial(_inproj_kernel, d_in=d_in, normed=normed)
    return pl.pallas_call(
        kern,
        grid=(t // tm,),
        in_specs=[
            pl.BlockSpec((tm, d), lambda i: (i, 0)),
            _resident(gmix.shape),
            _resident(w_cat.shape),
            _resident(wvt.shape),
            _resident(b_gate.shape),
            _resident(gain.shape),
            _resident(bd.shape),
        ],
        out_specs=[
            pl.BlockSpec((tm, d_in), lambda i: (i, 0)),
            pl.BlockSpec((tm, d_gate), lambda i: (i, 0)),
            pl.BlockSpec((1, d_v, tm), lambda i: (i, 0, 0)),
        ],
        out_shape=[
            jax.ShapeDtypeStruct((t, d_in), BF16),
            jax.ShapeDtypeStruct((t, d_gate), BF16),
            jax.ShapeDtypeStruct((t // tm, d_v, tm), BF16),
        ],
        compiler_params=_params(1),
        name="inproj",
    )(x2, gmix, w_cat, wvt, b_gate, gain, bd)


def _natten_kernel(flag_ref, shift_ref, q_ref, k_ref, v_ref, e_ref, o_ref, *, groups, n_rows):
    gblk = pl.program_id(2)
    gq = ROW_GROUP * GRID_W
    win = KEY_ROWS * GRID_W
    lane = lax.broadcasted_iota(jnp.int32, (gq, LANES), 1)
    first = lane < HEAD_DIM
    first_k = lax.broadcasted_iota(jnp.int32, (win, LANES), 1) < HEAD_DIM
    ones = jnp.where((lane[:1] % HEAD_DIM == N_POS_FEATS) | (lane[:1] % HEAD_DIM == N_POS_FEATS + 1),
                     1.0, 0.0).astype(BF16)
    def group(gi, exact_max):
        g = gblk * groups + gi
        r0 = g * ROW_GROUP
        ws = jnp.clip(r0 - NA_KH // 2, 0, n_rows - KEY_ROWS)
        cls = jnp.where(r0 == 0, 0, jnp.where(r0 == n_rows - ROW_GROUP, 2, 1))
        q = q_ref[gi * gq:(gi + 1) * gq, :]
        k0 = pl.multiple_of(ws * GRID_W, GRID_W)
        kw = k_ref[pl.ds(k0, win), :]
        vw = v_ref[pl.ds(k0, win), :]
        kones = jnp.broadcast_to(ones, kw.shape)
        ks = (jnp.where(first_k, kw, kones), jnp.where(first_k, kones, kw))
        f = jnp.zeros_like(q) if exact_max else jnp.broadcast_to(shift_ref[0], q.shape)
        qs = (jnp.where(first, q, f), jnp.where(first, f, q))
        outs = []
        for hh in range(2):
            s = lax.dot_general(qs[hh], ks[hh], (((1,), (1,)), ((), ())),
                                preferred_element_type=F32) + e_ref[hh, cls]
            if exact_max:
                s = s - jnp.max(s, axis=-1, keepdims=True)
            p = jnp.exp2(s)
            l = jnp.sum(p, axis=-1, keepdims=True)
            outs.append(jnp.dot(p.astype(BF16), vw, preferred_element_type=F32) / l)
        o_ref[gi * gq:(gi + 1) * gq, :] = jnp.where(first, outs[0], outs[1]).astype(BF16)

    needs_max = flag_ref[0] != 0

    @pl.when(jnp.logical_not(needs_max))
    def _():
        for gi in range(groups):
            group(gi, False)

    @pl.when(needs_max)
    def _():
        for gi in range(groups):
            group(gi, True)


def _natten(proj, e_tab, flag, shifts, *, batch, seq, n_pairs, q_blk, k_blk, v_blk, groups):
    n_rows = seq // GRID_W
    rb = groups * ROW_GROUP
    assert n_rows >= KEY_ROWS and n_rows % rb == 0
    steps = n_rows // rb
    kern = functools.partial(_natten_kernel, groups=groups, n_rows=n_rows)
    return pl.pallas_call(
        kern,
        grid=(n_pairs, batch, steps),
        in_specs=[
            pl.BlockSpec(memory_space=pltpu.SMEM),
            pl.BlockSpec((1, 1, LANES), lambda hp, b, r: (hp, 0, 0)),
            pl.BlockSpec((rb * GRID_W, LANES), lambda hp, b, r: (b * steps + r, q_blk + hp)),
            pl.BlockSpec((seq, LANES), lambda hp, b, r: (b, k_blk + hp)),
            pl.BlockSpec((seq, LANES), lambda hp, b, r: (b, v_blk + hp)),
            pl.BlockSpec((2,) + e_tab.shape[1:], lambda hp, b, r: (hp, 0, 0, 0)),
        ],
        out_specs=pl.BlockSpec((rb * GRID_W, LANES), lambda hp, b, r: (b * steps + r, hp)),
        out_shape=jax.ShapeDtypeStruct((batch * seq, n_pairs * LANES), BF16),
        compiler_params=_params(3),
        name="natten",
    )(flag, shifts, proj, proj, proj, e_tab)


POS_RADIX = 64
N_POS_FEATS = 8
MAX_SHIFT_GAP = 90.0
BOUND_SLACK = 2.0 ** -6


def _split_bf16_np(x):
    hi = np.asarray(x, np.float32).astype(BF16).astype(np.float32)
    return hi, np.float32(x - hi)


def _pos_feats(seq, slopes):
    pos = np.arange(seq)
    hi = (POS_RADIX * (pos // POS_RADIX)).astype(np.float32)
    lo = (pos % POS_RADIX).astype(np.float32)
    one = np.ones(seq, np.float32)
    zero = np.zeros(seq, np.float32)
    qs, ks = [], []
    for slope in np.asarray(slopes, np.float32):
        s_hi, s_lo = (float(v) for v in _split_bf16_np(np.float32(slope)))
        qs.append(np.stack([hi, hi, lo, lo, s_hi * one, s_lo * one, s_hi * one, s_lo * one,
                            zero, zero], axis=1))
        ks.append(np.stack([s_hi * one, s_lo * one, s_hi * one, s_lo * one, -hi, -hi, -lo, -lo,
                            one, one], axis=1))

    def table(feats):
        half = np.zeros((len(feats), seq, HEAD_DIM), np.float32)
        half[:, :, :feats[0].shape[1]] = np.stack(feats)
        return jnp.asarray(np.concatenate([half, half], axis=2), BF16)

    return table(qs), table(ks)


def _split_bf16(x):
    hi = x.astype(BF16).astype(F32)
    return hi, x - hi


def _shift_feats(shift, tq):
    lane = lax.broadcasted_iota(jnp.int32, (tq, LANES), 1) % HEAD_DIM
    hi, lo = _split_bf16(-shift)
    return jnp.where(lane == N_POS_FEATS, hi, jnp.where(lane == N_POS_FEATS + 1, lo, 0.0))


def _shift_bound(gain_q, gain_k, q_scale):
    bound = HEAD_DIM * jnp.max(jnp.abs(gain_q)) * jnp.max(jnp.abs(gain_k)) * q_scale
    return bound * (1.0 + BOUND_SLACK) + BOUND_SLACK


def _shift_table(shifts):
    n = shifts.shape[0]
    hi, lo = _split_bf16(-shifts)
    tab = jnp.zeros((n, 2, HEAD_DIM), F32)
    tab = tab.at[:, :, N_POS_FEATS].set(hi[:, ::-1]).at[:, :, N_POS_FEATS + 1].set(lo[:, ::-1])
    return tab.reshape(n, 1, LANES).astype(BF16)


def _diffattn_kernel(slopes_ref, flag_ref, lamp_ref, dist_ref, shift_ref, qf_ref, kf_ref,
                     q_ref, k_ref, vt_ref, subg_ref, o_ref, m_sc, l_sc, acc_sc,
                     *, tq, tk, seq):
    h = pl.program_id(1)
    qi = pl.program_id(2)
    q = q_ref[...]
    lane = lax.broadcasted_iota(jnp.int32, (tq, LANES), 1)
    first = lane < HEAD_DIM
    n_chunks = seq // tk
    n_slabs = tk // LANES
    posf = qf_ref[0].astype(F32)
    diag_bias = dist_ref[...] * (-slopes_ref[h])

    def operands(feats):
        f = feats.astype(BF16)
        return jnp.where(first, q, f), jnp.where(first, f, q)

    def for_chunks(shiftf, consume, unroll, keys_major):
        q_diag = operands(shiftf)
        q_after = operands(posf + shiftf)
        q_before = operands(shiftf - posf)

        def chunk(c, qs, bias):
            k0 = pl.multiple_of(c * tk, tk)
            kc = k_ref[pl.ds(k0, tk), :]
            kf = kf_ref[0, pl.ds(k0, tk), :]
            ks = (jnp.where(first, kc, kf), jnp.where(first, kf, kc))
            ss = []
            for mi in range(2):
                lhs, rhs = (ks[mi], qs[mi]) if keys_major else (qs[mi], ks[mi])
                s = lax.dot_general(lhs, rhs, (((1,), (1,)), ((), ())),
                                    preferred_element_type=F32)
                ss.append(s if bias is None else s + bias)
            consume(c, ss)

        chunk(qi, q_diag, diag_bias)

        def body(d, carry):
            c = lax.rem(qi + d, n_chunks)
            after = c > qi
            qs = tuple(jnp.where(after, q_after[mi], q_before[mi]) for mi in range(2))
            chunk(c, qs, None)
            return carry

        lax.fori_loop(1, n_chunks, body, 0, unroll=unroll)

    def attend(shiftf, unroll):
        l_sc[...] = jnp.zeros(l_sc.shape, F32)
        acc_sc[...] = jnp.zeros(acc_sc.shape, F32)

        def consume(c, ss):
            vt = vt_ref[c]
            for mi in range(2):
                p = jnp.exp2(ss[mi])
                l_sc[mi] += jnp.sum(p.reshape(tk // HALO, HALO, tq), axis=0)
                acc_sc[mi] += jnp.dot(vt, p.astype(BF16), preferred_element_type=F32)

        for_chunks(shiftf, consume, unroll, True)

    needs_max = flag_ref[0] != 0

    @pl.when(jnp.logical_not(needs_max))
    def _():
        attend(jnp.broadcast_to(shift_ref[0].astype(F32), (tq, LANES)), True)

    @pl.when(needs_max)
    def _():
        m_sc[...] = jnp.full(m_sc.shape, NEG_BIG, F32)

        def consume(c, ss):
            for mi in range(2):
                m = m_sc[mi]
                for j in range(n_slabs):
                    m = jnp.maximum(m, ss[mi][:, j * LANES:(j + 1) * LANES])
                m_sc[mi] = m

        for_chunks(jnp.zeros((tq, LANES), F32), consume, False, False)
        rowmax = [jnp.max(m_sc[mi], axis=-1, keepdims=True) for mi in range(2)]
        attend(_shift_feats(jnp.where(first, rowmax[1], rowmax[0]), tq), False)

    lp = lamp_ref[...]
    lam = (jnp.exp(jnp.sum(lp[0:1] * lp[1:2], axis=-1, keepdims=True))
           - jnp.exp(jnp.sum(lp[2:3] * lp[3:4], axis=-1, keepdims=True)) + LAM_INIT)
    ys = [acc_sc[mi] / jnp.sum(l_sc[mi], axis=0, keepdims=True) for mi in range(2)]
    yt = ys[0] - lam * ys[1]
    yt = yt * lax.rsqrt(jnp.mean(yt * yt, axis=0, keepdims=True) + EPS)
    o_ref[...] = (yt.T * subg_ref[...] * (1.0 - LAM_INIT)).astype(BF16)


def _diffattn(proj, vt, flag, shift, slopes, lam_params, subg, *, batch, seq, n_heads, q_blk,
              k_blk, tq, tk):
    steps = seq // tq
    assert tq == tk and seq % tk == 0 and seq <= POS_RADIX * 256
    assert vt.shape == (batch * seq // tk, n_heads * LANES, tk)
    qfeat, kfeat = _pos_feats(seq, slopes)
    idx = np.arange(tq)
    dist = jnp.asarray(np.abs(idx[:, None] - idx[None, :]), F32)
    kern = functools.partial(_diffattn_kernel, tq=tq, tk=tk, seq=seq)
    return pl.pallas_call(
        kern,
        grid=(batch, n_heads, steps),
        in_specs=[
            pl.BlockSpec(memory_space=pltpu.SMEM),
            pl.BlockSpec(memory_space=pltpu.SMEM),
            pl.BlockSpec(lam_params.shape, lambda b, h, i: (0, 0)),
            _resident(dist.shape),
            _resident(shift.shape),
            pl.BlockSpec((1, tq, LANES), lambda b, h, i: (h, i, 0)),
            pl.BlockSpec((1, seq, LANES), lambda b, h, i: (h, 0, 0)),
            pl.BlockSpec((tq, LANES), lambda b, h, i: (b * steps + i, q_blk + h)),
            pl.BlockSpec((seq, LANES), lambda b, h, i: (b, k_blk + h)),
            pl.BlockSpec((seq // tk, LANES, tk), lambda b, h, i: (b, h, 0)),
            pl.BlockSpec(subg.shape, lambda b, h, i: (0, 0)),
        ],
        out_specs=pl.BlockSpec((tq, LANES), lambda b, h, i: (b * steps + i, h)),
        out_shape=jax.ShapeDtypeStruct((batch * seq, n_heads * LANES), BF16),
        scratch_shapes=[
            pltpu.VMEM((2, tq, LANES), F32),
            pltpu.VMEM((2, HALO, tq), F32),
            pltpu.VMEM((2, LANES, tq), F32),
        ],
        compiler_params=_params(3),
        name="diffattn",
    )(jnp.asarray(slopes, F32), flag, lam_params, dist, shift, qfeat, kfeat,
      proj, proj, vt, subg)


def _mix_kernel(x_ref, ya_ref, yb_ref, g_ref, wa_ref, wb_ref, wo_ref, o_ref):
    d = x_ref.shape[1]
    pa = jnp.dot(ya_ref[...], wa_ref[...], preferred_element_type=F32)
    pb = jnp.dot(yb_ref[...], wb_ref[...], preferred_element_type=F32)
    mixed = g_ref[:, :d].astype(F32) * pa + g_ref[:, d:].astype(F32) * pb
    o_ref[...] = x_ref[...] + jnp.dot(mixed.astype(BF16), wo_ref[...],
                                      preferred_element_type=F32)


def _mix(x2, ya, yb, gate, wa, wb, wo, *, tm):
    t, d = x2.shape
    return pl.pallas_call(
        _mix_kernel,
        grid=(t // tm,),
        in_specs=[
            pl.BlockSpec((tm, d), lambda i: (i, 0)),
            pl.BlockSpec((tm, ya.shape[1]), lambda i: (i, 0)),
            pl.BlockSpec((tm, yb.shape[1]), lambda i: (i, 0)),
            pl.BlockSpec((tm, gate.shape[1]), lambda i: (i, 0)),
            _resident(wa.shape),
            _resident(wb.shape),
            _resident(wo.shape),
        ],
        out_specs=pl.BlockSpec((tm, d), lambda i: (i, 0)),
        out_shape=jax.ShapeDtypeStruct((t, d), F32),
        compiler_params=_params(1),
        name="mix",
    )(x2, ya, yb, gate, wa, wb, wo)


FF_CHUNK = 256
FF_SLOTS = 2


def _ffn_kernel(x_ref, xp_ref, xn_ref, p_ref, gffn_ref, wup_ref, cw_ref, cb_ref, wdn_ref,
                gple_ref, wpg_ref, wpp_ref, o_ref, acc_sc, h_sc, *slots, tm, seq, d_ff):
    uv_sc, ug_sc, a_sc = (slots[i * FF_SLOTS:(i + 1) * FF_SLOTS] for i in range(3))
    i = pl.program_id(0)
    tiles_per_seq = seq // tm
    pos = i % tiles_per_seq
    keep_prev = (pos != 0).astype(F32)
    keep_next = (pos != tiles_per_seq - 1).astype(F32)
    x = x_ref[...]
    g = gffn_ref[...]

    def norm(v):
        return v * _rms_scale(v) * g

    h_sc[...] = jnp.concatenate(
        [norm(xp_ref[...]) * keep_prev, norm(x), norm(xn_ref[...]) * keep_next],
        axis=0).astype(BF16)
    acc_sc[...] = jnp.zeros(acc_sc.shape, F32)
    n_chunks = d_ff // FF_CHUNK

    def conv(u_ref, c):
        w = cw_ref[c]
        return (w[0:1] * u_ref[HALO - 1:HALO - 1 + tm, :] + w[1:2] * u_ref[HALO:HALO + tm, :]
                + w[2:3] * u_ref[HALO + 1:HALO + 1 + tm, :] + cb_ref[c])

    def up(c, slot):
        uv_sc[slot][...] = jnp.dot(h_sc[...], wup_ref[c], preferred_element_type=F32)
        ug_sc[slot][...] = jnp.dot(h_sc[...], wup_ref[n_chunks + c], preferred_element_type=F32)

    def act(c, slot):
        uv = conv(uv_sc[slot], c)
        ug = conv(ug_sc[slot], n_chunks + c)
        a_sc[slot][...] = (jax.nn.gelu(ug, approximate=True) * uv).astype(BF16)

    def down(c, slot):
        rows = pl.ds(pl.multiple_of(c * FF_CHUNK, FF_CHUNK), FF_CHUNK)
        acc_sc[...] += jnp.dot(a_sc[slot][...], wdn_ref[rows, :], preferred_element_type=F32)

    up(0, 0)
    up(1, 1)
    act(0, 0)
    for c in range(n_chunks):
        if c + 2 < n_chunks:
            up(c + 2, (c + 2) % FF_SLOTS)
        if c + 1 < n_chunks:
            act(c + 1, (c + 1) % FF_SLOTS)
        down(c, c % FF_SLOTS)

    x2 = x + acc_sc[...]
    h3 = (x2 * _rms_scale(x2) * gple_ref[...]).astype(BF16)
    pg = jax.nn.sigmoid(jnp.dot(h3, wpg_ref[...], preferred_element_type=F32))
    pp = jnp.dot(p_ref[...].astype(BF16), wpp_ref[...], preferred_element_type=F32)
    o_ref[...] = x2 + pg * pp


def _ffn(x1, p2, gffn, wup, cw, cb, wdn, gple, wpg, wpp, *, seq, tm):
    t, d = x1.shape
    d_ff = wdn.shape[0]
    assert seq % tm == 0 and tm % HALO == 0 and d_ff % FF_CHUNK == 0
    hb = tm // HALO
    last = t // HALO - 1
    kern = functools.partial(_ffn_kernel, tm=tm, seq=seq, d_ff=d_ff)
    return pl.pallas_call(
        kern,
        grid=(t // tm,),
        in_specs=[
            pl.BlockSpec((tm, d), lambda i: (i, 0)),
            pl.BlockSpec((HALO, d), lambda i: (jnp.maximum(i * hb - 1, 0), 0)),
            pl.BlockSpec((HALO, d), lambda i: (jnp.minimum((i + 1) * hb, last), 0)),
            pl.BlockSpec((tm, p2.shape[1]), lambda i: (i, 0)),
            _resident(gffn.shape),
            _resident(wup.shape),
            _resident(cw.shape),
            _resident(cb.shape),
            _resident(wdn.shape),
            _resident(gple.shape),
            _resident(wpg.shape),
            _resident(wpp.shape),
        ],
        out_specs=pl.BlockSpec((tm, d), lambda i: (i, 0)),
        out_shape=jax.ShapeDtypeStruct((t, d), F32),
        scratch_shapes=[
            pltpu.VMEM((tm, d), F32),
            pltpu.VMEM((tm + 2 * HALO, d), BF16),
        ] + [pltpu.VMEM((tm + 2 * HALO, FF_CHUNK), F32)] * (2 * FF_SLOTS)
          + [pltpu.VMEM((tm, FF_CHUNK), BF16)] * FF_SLOTS,
        compiler_params=_params(1),
        name="ffn",
    )(x1, x1, x1, p2, gffn, wup, cw, cb, wdn, gple, wpg, wpp)


def kernel(x, p, norm_mix_g, w_in, qn_a_q, qn_a_k, rpb, qn_b_q, qn_b_k, lam_q1, lam_k1, lam_q2, lam_k2, subln_g, w_proj_a, w_proj_b, w_gate, b_gate, w_out, norm_ffn_g, w_up, conv_w, conv_b, w_down, norm_ple_g, w_ple_gate, w_ple_proj):
    batch, seq, d = x.shape
    depth = p.shape[0]
    assert depth == 1
    t = batch * seq
    width_a = w_proj_a.shape[1]
    width_b = w_proj_b.shape[1]
    n_heads_a = width_a // HEAD_DIM
    n_heads_b = width_b // (2 * HEAD_DIM)
    d_in = 3 * width_a + 2 * width_b
    assert width_a == CHUNK and width_b == CHUNK

    scale = 1.0 / math.sqrt(HEAD_DIM)
    ones = jnp.ones((CHUNK,), F32)
    gain = jnp.concatenate([
        jnp.tile(qn_a_q[0], n_heads_a) * (scale * LOG2E), jnp.tile(qn_a_k[0], n_heads_a), ones,
        jnp.tile(qn_b_q[0], 2 * n_heads_b) * (scale * LOG2E), jnp.tile(qn_b_k[0], 2 * n_heads_b),
    ])[None, :]
    normed = (True, True, False, True, True)
    bd = jnp.asarray(np.kron(np.eye(SLAB // HEAD_DIM), np.full((HEAD_DIM, HEAD_DIM), 1.0 / HEAD_DIM)),
                     BF16)
    w_cat = jnp.concatenate([w_in[0][:, :d_in], w_gate[0]], axis=1).astype(BF16)
    wvt = w_in[0][:, d_in:].T.astype(BF16)

    x2 = x.reshape(t, d)
    proj, gate, vt = _inproj(x2, norm_mix_g, w_cat, wvt, b_gate, gain, bd,
                             d_in=d_in, normed=normed, tm=512)

    e_tab = _bias_table(rpb[0])
    blk = CHUNK // LANES
    n_pairs = n_heads_a // 2
    bound_a = _shift_bound(qn_a_q[0], qn_a_k[0], scale * LOG2E)
    b_hi = jnp.max(rpb[0], axis=(1, 2)) * LOG2E
    b_self = rpb[0][:, NA_KH - 1, NA_KW - 1] * LOG2E
    gap_a = 2.0 * bound_a + jnp.max(b_hi - b_self)
    flag_a = jnp.logical_not(gap_a <= MAX_SHIFT_GAP).astype(jnp.int32).reshape(1)
    ya = _natten(proj, e_tab, flag_a, _shift_table((bound_a + b_hi).reshape(n_pairs, 2)),
                 batch=batch, seq=seq, n_pairs=n_pairs, q_blk=0, k_blk=blk, v_blk=2 * blk,
                 groups=4)

    slopes = (LOG2E * 2.0 ** (-8.0 * (np.arange(n_heads_b) + 1.0) / n_heads_b)).astype(np.float32)
    lam_params = jnp.concatenate([lam_q1, lam_k1, lam_q2, lam_k2], axis=0)
    bound_b = _shift_bound(qn_b_q[0], qn_b_k[0], scale * LOG2E)
    flag_b = jnp.logical_not(2.0 * bound_b <= MAX_SHIFT_GAP).astype(jnp.int32).reshape(1)
    shift_b = _shift_table(jnp.broadcast_to(bound_b, (1, 2)))
    yb = _diffattn(proj, vt, flag_b, shift_b, slopes, lam_params, subln_g, batch=batch, seq=seq, n_heads=n_heads_b,
                   q_blk=3 * blk, k_blk=4 * blk, tq=512, tk=512)

    x1 = _mix(x2, ya, yb, gate, w_proj_a[0].astype(BF16), w_proj_b[0].astype(BF16),
              w_out[0].astype(BF16), tm=512)

    n_col = w_up.shape[2] // FF_CHUNK
    wup3 = w_up[0].astype(BF16).reshape(d, n_col, FF_CHUNK).transpose(1, 0, 2)
    cw3 = conv_w[0].reshape(CONV_W, n_col, FF_CHUNK).transpose(1, 0, 2)
    cb3 = conv_b.reshape(n_col, 1, FF_CHUNK)
    out = _ffn(x1, p[0].reshape(t, -1), norm_ffn_g, wup3, cw3, cb3,
               w_down[0].astype(BF16), norm_ple_g, w_ple_gate[0].astype(BF16),
               w_ple_proj[0].astype(BF16), seq=seq, tm=512)
    return out.reshape(batch, seq, d)
```

```python
import functools
import math

import jax
import jax.numpy as jnp
import numpy as np
from jax import lax
from jax.experimental import pallas as pl
from jax.experimental.pallas import tpu as pltpu

F32 = jnp.float32
BF16 = jnp.bfloat16

HEAD_DIM = 64
GRID_W = 64
NA_KH = 8
NA_KW = 16
CONV_W = 3
EPS = 1e-6
LAM_INIT = 0.8 - 0.6 * math.exp(-0.3 * 0)
NEG_BIG = -1e30
LOG2E = math.log2(math.e)

LANES = 128
HALO = 8
VMEM_LIMIT = 56 * 1024 * 1024


def _params(n_axes, vmem=VMEM_LIMIT):
    return pltpu.CompilerParams(
        dimension_semantics=("arbitrary",) * n_axes, vmem_limit_bytes=vmem)


def _resident(shape):
    nd = len(shape)
    return pl.BlockSpec(shape, lambda *_: (0,) * nd, pipeline_mode=pl.Buffered(1))


def _rms_scale(x):
    return lax.rsqrt(jnp.mean(x * x, axis=-1, keepdims=True) + EPS)


ROW_GROUP = 4
KEY_ROWS = 12
_GROUP_CLASSES = (
    (0, lambda i: 0),
    (-(NA_KH // 2), lambda i: i - NA_KH // 2),
    (ROW_GROUP - KEY_ROWS, lambda i: ROW_GROUP - NA_KH),
)


def _bias_table_kernel(rpb_ref, e_ref):
    h = pl.program_id(0)
    n_dr = 2 * NA_KH - 1
    n_dc = 2 * NA_KW - 1
    cq = lax.broadcasted_iota(jnp.int32, (GRID_W, LANES), 0)
    lane = lax.broadcasted_iota(jnp.int32, (GRID_W, LANES), 1)
    ck = lane % GRID_W
    upper = lane >= GRID_W
    dc = jnp.clip(ck - cq, -(NA_KW - 1), NA_KW - 1) + (NA_KW - 1)
    cs = jnp.clip(cq - NA_KW // 2, 0, GRID_W - NA_KW)
    inside = (ck >= cs) & (ck < cs + NA_KW)
    neg = jnp.full((GRID_W, LANES), NEG_BIG, F32)
    tiles = []
    for dr in range(n_dr):
        acc = neg
        for d in range(n_dc):
            acc = jnp.where(inside & (dc == d), rpb_ref[h * n_dr + dr, d] * LOG2E, acc)
        tiles.append(acc)

    def half(ws_off, rs_off, i, j):
        row = ws_off + j
        if rs_off <= row < rs_off + NA_KH:
            return tiles[row - i + NA_KH - 1]
        return neg

    for cls, (ws_off, rs_fn) in enumerate(_GROUP_CLASSES):
        for i in range(ROW_GROUP):
            for j in range(0, KEY_ROWS, 2):
                lo = half(ws_off, rs_fn(i), i, j)
                hi = half(ws_off, rs_fn(i), i, j + 1)
                e_ref[0, cls, i * GRID_W:(i + 1) * GRID_W, j * GRID_W:(j + 2) * GRID_W] = (
                    jnp.where(upper, hi, lo))


def _bias_table(rpb):
    n_heads = rpb.shape[0]
    rpb2 = rpb.reshape(n_heads * (2 * NA_KH - 1), 2 * NA_KW - 1)
    shape = (n_heads, len(_GROUP_CLASSES), ROW_GROUP * GRID_W, KEY_ROWS * GRID_W)
    return pl.pallas_call(
        _bias_table_kernel,
        grid=(n_heads,),
        in_specs=[pl.BlockSpec(memory_space=pltpu.SMEM)],
        out_specs=pl.BlockSpec((1,) + shape[1:], lambda h: (h, 0, 0, 0)),
        out_shape=jax.ShapeDtypeStruct(shape, F32),
        compiler_params=_params(1),
        name="bias_table",
    )(rpb2)


CHUNK = 512
SLAB = 256


def _inproj_kernel(x_ref, gmix_ref, w_ref, wg_ref, wvt_ref, bgate_ref, gain_ref, bd_ref,
                   proj_ref, gate_ref, vt_ref, *, d_in, normed):
    x = x_ref[...]
    h = (x * _rms_scale(x) * gmix_ref[...]).astype(BF16)
    vt_ref[0] = lax.dot_general(wvt_ref[...], h, (((1,), (1,)), ((), ())),
                                preferred_element_type=F32).astype(BF16)
    for c in range(d_in // CHUNK):
        p = jnp.dot(h, w_ref[:, c * CHUNK:(c + 1) * CHUNK], preferred_element_type=F32)
        if normed[c]:
            for s in range(CHUNK // SLAB):
                col = slice(c * CHUNK + s * SLAB, c * CHUNK + (s + 1) * SLAB)
                ps = p[:, s * SLAB:(s + 1) * SLAB]
                ms = jnp.dot((ps * ps).astype(BF16), bd_ref[...], preferred_element_type=F32)
                proj_ref[:, col] = (ps * lax.rsqrt(ms + EPS) * gain_ref[:, col]).astype(BF16)
        else:
            proj_ref[:, c * CHUNK:(c + 1) * CHUNK] = p.astype(BF16)
    d_gate = gate_ref.shape[1]
    for c in range(d_gate // CHUNK):
        col = slice(c * CHUNK, (c + 1) * CHUNK)
        z = jnp.dot(h, wg_ref[:, col], preferred_element_type=F32) + bgate_ref[:, col]
        gate_ref[:, col] = jax.nn.sigmoid(z).astype(BF16)


def _inproj(x2, gmix, w_in, w_gate, wvt, b_gate, gain, bd, *, d_in, normed, tm):
    t, d = x2.shape
    d_gate = w_gate.shape[1]
    d_v = wvt.shape[0]
    kern = functools.partial(_inproj_kernel, d_in=d_in, normed=normed)
    return pl.pallas_call(
        kern,
        grid=(t // tm,),
        in_specs=[
            pl.BlockSpec((tm, d), lambda i: (i, 0)),
            _resident(gmix.shape),
            _resident(w_in.shape),
            _resident(w_gate.shape),
            _resident(wvt.shape),
            _resident(b_gate.shape),
            _resident(gain.shape),
            _resident(bd.shape),
        ],
        out_specs=[
            pl.BlockSpec((tm, d_in), lambda i: (i, 0)),
            pl.BlockSpec((tm, d_gate), lambda i: (i, 0)),
            pl.BlockSpec((1, d_v, tm), lambda i: (i, 0, 0)),
        ],
        out_shape=[
            jax.ShapeDtypeStruct((t, d_in), BF16),
            jax.ShapeDtypeStruct((t, d_gate), BF16),
            jax.ShapeDtypeStruct((t // tm, d_v, tm), BF16),
        ],
        compiler_params=_params(1),
        name="inproj",
    )(x2, gmix, w_in, w_gate, wvt, b_gate, gain, bd)


def _natten_kernel(flag_ref, shift_ref, q_ref, k_ref, v_ref, e_ref, o_ref, *, groups, n_rows):
    gblk = pl.program_id(2)
    gq = ROW_GROUP * GRID_W
    win = KEY_ROWS * GRID_W
    lane = lax.broadcasted_iota(jnp.int32, (gq, LANES), 1)
    first = lane < HEAD_DIM
    first_k = lax.broadcasted_iota(jnp.int32, (win, LANES), 1) < HEAD_DIM
    ones = jnp.where((lane[:1] % HEAD_DIM == N_POS_FEATS) | (lane[:1] % HEAD_DIM == N_POS_FEATS + 1),
                     1.0, 0.0).astype(BF16)
    def group(gi, exact_max):
        g = gblk * groups + gi
        r0 = g * ROW_GROUP
        ws = jnp.clip(r0 - NA_KH // 2, 0, n_rows - KEY_ROWS)
        cls = jnp.where(r0 == 0, 0, jnp.where(r0 == n_rows - ROW_GROUP, 2, 1))
        q = q_ref[gi * gq:(gi + 1) * gq, :]
        k0 = pl.multiple_of(ws * GRID_W, GRID_W)
        kw = k_ref[pl.ds(k0, win), :]
        vw = v_ref[pl.ds(k0, win), :]
        kones = jnp.broadcast_to(ones, kw.shape)
        ks = (jnp.where(first_k, kw, kones), jnp.where(first_k, kones, kw))
        f = jnp.zeros_like(q) if exact_max else jnp.broadcast_to(shift_ref[0], q.shape)
        qs = (jnp.where(first, q, f), jnp.where(first, f, q))
        outs = []
        for hh in range(2):
            s = lax.dot_general(qs[hh], ks[hh], (((1,), (1,)), ((), ())),
                                preferred_element_type=F32) + e_ref[hh, cls]
            if exact_max:
                s = s - jnp.max(s, axis=-1, keepdims=True)
            p = jnp.exp2(s)
            l = jnp.sum(p, axis=-1, keepdims=True)
            outs.append(jnp.dot(p.astype(BF16), vw, preferred_element_type=F32) / l)
        o_ref[gi * gq:(gi + 1) * gq, :] = jnp.where(first, outs[0], outs[1]).astype(BF16)

    needs_max = flag_ref[0] != 0

    @pl.when(jnp.logical_not(needs_max))
    def _():
        for gi in range(groups):
            group(gi, False)

    @pl.when(needs_max)
    def _():
        for gi in range(groups):
            group(gi, True)


def _natten(proj, e_tab, flag, shifts, *, batch, seq, n_pairs, q_blk, k_blk, v_blk, groups):
    n_rows = seq // GRID_W
    rb = groups * ROW_GROUP
    assert n_rows >= KEY_ROWS and n_rows % rb == 0
    steps = n_rows // rb
    kern = functools.partial(_natten_kernel, groups=groups, n_rows=n_rows)
    return pl.pallas_call(
        kern,
        grid=(n_pairs, batch, steps),
        in_specs=[
            pl.BlockSpec(memory_space=pltpu.SMEM),
            pl.BlockSpec((1, 1, LANES), lambda hp, b, r: (hp, 0, 0)),
            pl.BlockSpec((rb * GRID_W, LANES), lambda hp, b, r: (b * steps + r, q_blk + hp)),
            pl.BlockSpec((seq, LANES), lambda hp, b, r: (b, k_blk + hp)),
            pl.BlockSpec((seq, LANES), lambda hp, b, r: (b, v_blk + hp)),
            pl.BlockSpec((2,) + e_tab.shape[1:], lambda hp, b, r: (hp, 0, 0, 0)),
        ],
        out_specs=pl.BlockSpec((rb * GRID_W, LANES), lambda hp, b, r: (b * steps + r, hp)),
        out_shape=jax.ShapeDtypeStruct((batch * seq, n_pairs * LANES), BF16),
        compiler_params=_params(3),
        name="natten",
    )(flag, shifts, proj, proj, proj, e_tab)


POS_RADIX = 64
N_POS_FEATS = 8
MAX_SHIFT_GAP = 90.0
BOUND_SLACK = 2.0 ** -6


def _split_bf16_np(x):
    hi = np.asarray(x, np.float32).astype(BF16).astype(np.float32)
    return hi, np.float32(x - hi)


def _pos_feats(seq, slopes):
    pos = np.arange(seq)
    hi = (POS_RADIX * (pos // POS_RADIX)).astype(np.float32)
    lo = (pos % POS_RADIX).astype(np.float32)
    one = np.ones(seq, np.float32)
    zero = np.zeros(seq, np.float32)
    qs, ks = [], []
    for slope in np.asarray(slopes, np.float32):
        s_hi, s_lo = (float(v) for v in _split_bf16_np(np.float32(slope)))
        qs.append(np.stack([hi, hi, lo, lo, s_hi * one, s_lo * one, s_hi * one, s_lo * one,
                            zero, zero], axis=1))
        ks.append(np.stack([s_hi * one, s_lo * one, s_hi * one, s_lo * one, -hi, -hi, -lo, -lo,
                            one, one], axis=1))

    def table(feats):
        half = np.zeros((len(feats), seq, HEAD_DIM), np.float32)
        half[:, :, :feats[0].shape[1]] = np.stack(feats)
        return jnp.asarray(np.concatenate([half, half], axis=2), BF16)

    return table(qs), table(ks)


def _split_bf16(x):
    hi = x.astype(BF16).astype(F32)
    return hi, x - hi


def _shift_feats(shift, tq):
    lane = lax.broadcasted_iota(jnp.int32, (tq, LANES), 1) % HEAD_DIM
    hi, lo = _split_bf16(-shift)
    return jnp.where(lane == N_POS_FEATS, hi, jnp.where(lane == N_POS_FEATS + 1, lo, 0.0))


def _shift_bound(gain_q, gain_k, q_scale):
    bound = HEAD_DIM * jnp.max(jnp.abs(gain_q)) * jnp.max(jnp.abs(gain_k)) * q_scale
    return bound * (1.0 + BOUND_SLACK) + BOUND_SLACK


def _shift_table(shifts):
    n = shifts.shape[0]
    hi, lo = _split_bf16(-shifts)
    tab = jnp.zeros((n, 2, HEAD_DIM), F32)
    tab = tab.at[:, :, N_POS_FEATS].set(hi[:, ::-1]).at[:, :, N_POS_FEATS + 1].set(lo[:, ::-1])
    return tab.reshape(n, 1, LANES).astype(BF16)


def _diffattn_kernel(slopes_ref, flag_ref, lamp_ref, dist_ref, shift_ref, qf_ref, kf_ref,
                     q_ref, k_ref, vt_ref, subg_ref, o_ref, m_sc, l_sc, acc_sc,
                     *, tq, tk, seq):
    h = pl.program_id(1)
    qi = pl.program_id(2)
    q = q_ref[...]
    lane = lax.broadcasted_iota(jnp.int32, (tq, LANES), 1)
    first = lane < HEAD_DIM
    n_chunks = seq // tk
    n_slabs = tk // LANES
    posf = qf_ref[0].astype(F32)
    diag_bias = dist_ref[...] * (-slopes_ref[h])

    def operands(feats):
        f = feats.astype(BF16)
        return jnp.where(first, q, f), jnp.where(first, f, q)

    def for_chunks(shiftf, consume, unroll, keys_major):
        q_diag = operands(shiftf)
        q_after = operands(posf + shiftf)
        q_before = operands(shiftf - posf)

        def chunk(c, qs, bias):
            k0 = pl.multiple_of(c * tk, tk)
            kc = k_ref[pl.ds(k0, tk), :]
            kf = kf_ref[0, pl.ds(k0, tk), :]
            ks = (jnp.where(first, kc, kf), jnp.where(first, kf, kc))
            ss = []
            for mi in range(2):
                lhs, rhs = (ks[mi], qs[mi]) if keys_major else (qs[mi], ks[mi])
                s = lax.dot_general(lhs, rhs, (((1,), (1,)), ((), ())),
                                    preferred_element_type=F32)
                ss.append(s if bias is None else s + bias)
            consume(c, ss)

        chunk(qi, q_diag, diag_bias)

        def body(d, carry):
            c = lax.rem(qi + d, n_chunks)
            after = c > qi
            qs = tuple(jnp.where(after, q_after[mi], q_before[mi]) for mi in range(2))
            chunk(c, qs, None)
            return carry

        lax.fori_loop(1, n_chunks, body, 0, unroll=unroll)

    def attend(shiftf, unroll):
        l_sc[...] = jnp.zeros(l_sc.shape, F32)
        acc_sc[...] = jnp.zeros(acc_sc.shape, F32)

        def consume(c, ss):
            vt = vt_ref[c]
            for mi in range(2):
                p = jnp.exp2(ss[mi])
                l_sc[mi] += jnp.sum(p.reshape(tk // HALO, HALO, tq), axis=0)
                acc_sc[mi] += jnp.dot(vt, p.astype(BF16), preferred_element_type=F32)

        for_chunks(shiftf, consume, unroll, True)

    needs_max = flag_ref[0] != 0

    @pl.when(jnp.logical_not(needs_max))
    def _():
        attend(jnp.broadcast_to(shift_ref[0].astype(F32), (tq, LANES)), True)

    @pl.when(needs_max)
    def _():
        m_sc[...] = jnp.full(m_sc.shape, NEG_BIG, F32)

        def consume(c, ss):
            for mi in range(2):
                m = m_sc[mi]
                for j in range(n_slabs):
                    m = jnp.maximum(m, ss[mi][:, j * LANES:(j + 1) * LANES])
                m_sc[mi] = m

        for_chunks(jnp.zeros((tq, LANES), F32), consume, False, False)
        rowmax = [jnp.max(m_sc[mi], axis=-1, keepdims=True) for mi in range(2)]
        attend(_shift_feats(jnp.where(first, rowmax[1], rowmax[0]), tq), False)

    lp = lamp_ref[...]
    lam = (jnp.exp(jnp.sum(lp[0:1] * lp[1:2], axis=-1, keepdims=True))
           - jnp.exp(jnp.sum(lp[2:3] * lp[3:4], axis=-1, keepdims=True)) + LAM_INIT)
    ys = [acc_sc[mi] / jnp.sum(l_sc[mi], axis=0, keepdims=True) for mi in range(2)]
    yt = ys[0] - lam * ys[1]
    yt = yt * lax.rsqrt(jnp.mean(yt * yt, axis=0, keepdims=True) + EPS)
    o_ref[...] = (yt.T * subg_ref[...] * (1.0 - LAM_INIT)).astype(BF16)


def _diffattn(proj, vt, flag, shift, slopes, lam_params, subg, *, batch, seq, n_heads, q_blk,
              k_blk, tq, tk):
    steps = seq // tq
    assert tq == tk and seq % tk == 0 and seq <= POS_RADIX * 256
    assert vt.shape == (batch * seq // tk, n_heads * LANES, tk)
    qfeat, kfeat = _pos_feats(seq, slopes)
    idx = np.arange(tq)
    dist = jnp.asarray(np.abs(idx[:, None] - idx[None, :]), F32)
    kern = functools.partial(_diffattn_kernel, tq=tq, tk=tk, seq=seq)
    return pl.pallas_call(
        kern,
        grid=(batch, n_heads, steps),
        in_specs=[
            pl.BlockSpec(memory_space=pltpu.SMEM),
            pl.BlockSpec(memory_space=pltpu.SMEM),
            pl.BlockSpec(lam_params.shape, lambda b, h, i: (0, 0)),
            _resident(dist.shape),
            _resident(shift.shape),
            pl.BlockSpec((1, tq, LANES), lambda b, h, i: (h, i, 0)),
            pl.BlockSpec((1, seq, LANES), lambda b, h, i: (h, 0, 0)),
            pl.BlockSpec((tq, LANES), lambda b, h, i: (b * steps + i, q_blk + h)),
            pl.BlockSpec((seq, LANES), lambda b, h, i: (b, k_blk + h)),
            pl.BlockSpec((seq // tk, LANES, tk), lambda b, h, i: (b, h, 0)),
            pl.BlockSpec(subg.shape, lambda b, h, i: (0, 0)),
        ],
        out_specs=pl.BlockSpec((tq, LANES), lambda b, h, i: (b * steps + i, h)),
        out_shape=jax.ShapeDtypeStruct((batch * seq, n_heads * LANES), BF16),
        scratch_shapes=[
            pltpu.VMEM((2, tq, LANES), F32),
            pltpu.VMEM((2, HALO, tq), F32),
            pltpu.VMEM((2, LANES, tq), F32),
        ],
        compiler_params=_params(3),
        name="diffattn",
    )(jnp.asarray(slopes, F32), flag, lam_params, dist, shift, qfeat, kfeat,
      proj, proj, vt, subg)


def _mix_kernel(x_ref, ya_ref, yb_ref, g_ref, wa_ref, wb_ref, wo_ref, o_ref):
    d = x_ref.shape[1]
    pa = jnp.dot(ya_ref[...], wa_ref[...], preferred_element_type=F32)
    pb = jnp.dot(yb_ref[...], wb_ref[...], preferred_element_type=F32)
    mixed = g_ref[:, :d].astype(F32) * pa + g_ref[:, d:].astype(F32) * pb
    o_ref[...] = x_ref[...] + jnp.dot(mixed.astype(BF16), wo_ref[...],
                                      preferred_element_type=F32)


def _mix(x2, ya, yb, gate, wa, wb, wo, *, tm):
    t, d = x2.shape
    return pl.pallas_call(
        _mix_kernel,
        grid=(t // tm,),
        in_specs=[
            pl.BlockSpec((tm, d), lambda i: (i, 0)),
            pl.BlockSpec((tm, ya.shape[1]), lambda i: (i, 0)),
            pl.BlockSpec((tm, yb.shape[1]), lambda i: (i, 0)),
            pl.BlockSpec((tm, gate.shape[1]), lambda i: (i, 0)),
            _resident(wa.shape),
            _resident(wb.shape),
            _resident(wo.shape),
        ],
        out_specs=pl.BlockSpec((tm, d), lambda i: (i, 0)),
        out_shape=jax.ShapeDtypeStruct((t, d), F32),
        compiler_params=_params(1),
        name="mix",
    )(x2, ya, yb, gate, wa, wb, wo)


FF_CHUNK = 256
FF_SLOTS = 2


def _ffn_kernel(x_ref, xp_ref, xn_ref, p_ref, gffn_ref, wup_ref, cw_ref, cb_ref, wdn_ref,
                gple_ref, wpg_ref, wpp_ref, o_ref, acc_sc, h_sc, *slots, tm, seq, d_ff):
    uv_sc, ug_sc, a_sc = (slots[i * FF_SLOTS:(i + 1) * FF_SLOTS] for i in range(3))
    i = pl.program_id(0)
    tiles_per_seq = seq // tm
    pos = i % tiles_per_seq
    keep_prev = (pos != 0).astype(F32)
    keep_next = (pos != tiles_per_seq - 1).astype(F32)
    x = x_ref[...]
    g = gffn_ref[...]

    def norm(v):
        return v * _rms_scale(v) * g

    h_sc[...] = jnp.concatenate(
        [norm(xp_ref[...]) * keep_prev, norm(x), norm(xn_ref[...]) * keep_next],
        axis=0).astype(BF16)
    acc_sc[...] = jnp.zeros(acc_sc.shape, F32)
    n_chunks = d_ff // FF_CHUNK

    def cols(c):
        return slice(c * FF_CHUNK, (c + 1) * FF_CHUNK)

    def conv(u_ref, col):
        w = cw_ref[:, col]
        return (w[0:1] * u_ref[HALO - 1:HALO - 1 + tm, :] + w[1:2] * u_ref[HALO:HALO + tm, :]
                + w[2:3] * u_ref[HALO + 1:HALO + 1 + tm, :] + cb_ref[:, col])

    def up(c, slot):
        uv_sc[slot][...] = jnp.dot(h_sc[...], wup_ref[:, cols(c)], preferred_element_type=F32)
        ug_sc[slot][...] = jnp.dot(h_sc[...], wup_ref[:, cols(n_chunks + c)],
                                   preferred_element_type=F32)

    def act(c, slot):
        uv = conv(uv_sc[slot], cols(c))
        ug = conv(ug_sc[slot], cols(n_chunks + c))
        a_sc[slot][...] = (jax.nn.gelu(ug, approximate=True) * uv).astype(BF16)

    def down(c, slot):
        acc_sc[...] += jnp.dot(a_sc[slot][...], wdn_ref[cols(c), :], preferred_element_type=F32)

    up(0, 0)
    up(1, 1)
    act(0, 0)
    for c in range(n_chunks):
        if c + 2 < n_chunks:
            up(c + 2, (c + 2) % FF_SLOTS)
        if c + 1 < n_chunks:
            act(c + 1, (c + 1) % FF_SLOTS)
        down(c, c % FF_SLOTS)

    x2 = x + acc_sc[...]
    h3 = (x2 * _rms_scale(x2) * gple_ref[...]).astype(BF16)
    pg = jax.nn.sigmoid(jnp.dot(h3, wpg_ref[...], preferred_element_type=F32))
    pp = jnp.dot(p_ref[...].astype(BF16), wpp_ref[...], preferred_element_type=F32)
    o_ref[...] = x2 + pg * pp


def _ffn(x1, p2, gffn, wup, cw, cb, wdn, gple, wpg, wpp, *, seq, tm):
    t, d = x1.shape
    d_ff = wdn.shape[0]
    assert seq % tm == 0 and tm % HALO == 0 and d_ff % FF_CHUNK == 0
    hb = tm // HALO
    last = t // HALO - 1
    kern = functools.partial(_ffn_kernel, tm=tm, seq=seq, d_ff=d_ff)
    return pl.pallas_call(
        kern,
        grid=(t // tm,),
        in_specs=[
            pl.BlockSpec((tm, d), lambda i: (i, 0)),
            pl.BlockSpec((HALO, d), lambda i: (jnp.maximum(i * hb - 1, 0), 0)),
            pl.BlockSpec((HALO, d), lambda i: (jnp.minimum((i + 1) * hb, last), 0)),
            pl.BlockSpec((tm, p2.shape[1]), lambda i: (i, 0)),
            _resident(gffn.shape),
            _resident(wup.shape),
            _resident(cw.shape),
            _resident(cb.shape),
            _resident(wdn.shape),
            _resident(gple.shape),
            _resident(wpg.shape),
            _resident(wpp.shape),
        ],
        out_specs=pl.BlockSpec((tm, d), lambda i: (i, 0)),
        out_shape=jax.ShapeDtypeStruct((t, d), F32),
        scratch_shapes=[
            pltpu.VMEM((tm, d), F32),
            pltpu.VMEM((tm + 2 * HALO, d), BF16),
        ] + [pltpu.VMEM((tm + 2 * HALO, FF_CHUNK), F32)] * (2 * FF_SLOTS)
          + [pltpu.VMEM((tm, FF_CHUNK), BF16)] * FF_SLOTS,
        compiler_params=_params(1),
        name="ffn",
    )(x1, x1, x1, p2, gffn, wup, cw, cb, wdn, gple, wpg, wpp)


def kernel(x, p, norm_mix_g, w_in, qn_a_q, qn_a_k, rpb, qn_b_q, qn_b_k, lam_q1, lam_k1, lam_q2, lam_k2, subln_g, w_proj_a, w_proj_b, w_gate, b_gate, w_out, norm_ffn_g, w_up, conv_w, conv_b, w_down, norm_ple_g, w_ple_gate, w_ple_proj):
    batch, seq, d = x.shape
    depth = p.shape[0]
    assert depth == 1
    t = batch * seq
    width_a = w_proj_a.shape[1]
    width_b = w_proj_b.shape[1]
    n_heads_a = width_a // HEAD_DIM
    n_heads_b = width_b // (2 * HEAD_DIM)
    d_in = 3 * width_a + 2 * width_b
    assert width_a == CHUNK and width_b == CHUNK

    scale = 1.0 / math.sqrt(HEAD_DIM)
    ones = jnp.ones((CHUNK,), F32)
    gain = jnp.concatenate([
        jnp.tile(qn_a_q[0], n_heads_a) * (scale * LOG2E), jnp.tile(qn_a_k[0], n_heads_a), ones,
        jnp.tile(qn_b_q[0], 2 * n_heads_b) * (scale * LOG2E), jnp.tile(qn_b_k[0], 2 * n_heads_b),
    ])[None, :]
    normed = (True, True, False, True, True)
    bd = jnp.asarray(np.kron(np.eye(SLAB // HEAD_DIM), np.full((HEAD_DIM, HEAD_DIM), 1.0 / HEAD_DIM)),
                     BF16)
    wvt = w_in[0][:, d_in:].T.astype(BF16)

    x2 = x.reshape(t, d)
    proj, gate, vt = _inproj(x2, norm_mix_g, w_in[0].astype(BF16), w_gate[0].astype(BF16), wvt,
                             b_gate, gain, bd, d_in=d_in, normed=normed, tm=512)

    e_tab = _bias_table(rpb[0])
    blk = CHUNK // LANES
    n_pairs = n_heads_a // 2
    bound_a = _shift_bound(qn_a_q[0], qn_a_k[0], scale * LOG2E)
    b_hi = jnp.max(rpb[0], axis=(1, 2)) * LOG2E
    b_self = rpb[0][:, NA_KH - 1, NA_KW - 1] * LOG2E
    gap_a = 2.0 * bound_a + jnp.max(b_hi - b_self)
    flag_a = jnp.logical_not(gap_a <= MAX_SHIFT_GAP).astype(jnp.int32).reshape(1)
    ya = _natten(proj, e_tab, flag_a, _shift_table((bound_a + b_hi).reshape(n_pairs, 2)),
                 batch=batch, seq=seq, n_pairs=n_pairs, q_blk=0, k_blk=blk, v_blk=2 * blk,
                 groups=4)

    slopes = (LOG2E * 2.0 ** (-8.0 * (np.arange(n_heads_b) + 1.0) / n_heads_b)).astype(np.float32)
    lam_params = jnp.concatenate([lam_q1, lam_k1, lam_q2, lam_k2], axis=0)
    bound_b = _shift_bound(qn_b_q[0], qn_b_k[0], scale * LOG2E)
    flag_b = jnp.logical_not(2.0 * bound_b <= MAX_SHIFT_GAP).astype(jnp.int32).reshape(1)
    shift_b = _shift_table(jnp.broadcast_to(bound_b, (1, 2)))
    yb = _diffattn(proj, vt, flag_b, shift_b, slopes, lam_params, subln_g, batch=batch, seq=seq, n_heads=n_heads_b,
                   q_blk=3 * blk, k_blk=4 * blk, tq=512, tk=512)

    x1 = _mix(x2, ya, yb, gate, w_proj_a[0].astype(BF16), w_proj_b[0].astype(BF16),
              w_out[0].astype(BF16), tm=512)

    out = _ffn(x1, p[0].reshape(t, -1), norm_ffn_g, w_up[0].astype(BF16), conv_w[0], conv_b,
               w_down[0].astype(BF16), norm_ple_g, w_ple_gate[0].astype(BF16),
               w_ple_proj[0].astype(BF16), seq=seq, tm=512)
    return out.reshape(batch, seq, d)
```

```python
import functools
import math

import jax
import jax.numpy as jnp
import numpy as np
from jax import lax
from jax.experimental import pallas as pl
from jax.experimental.pallas import tpu as pltpu

F32 = jnp.float32
BF16 = jnp.bfloat16

HEAD_DIM = 64
GRID_W = 64
NA_KH = 8
NA_KW = 16
CONV_W = 3
EPS = 1e-6
LAM_INIT = 0.8 - 0.6 * math.exp(-0.3 * 0)
NEG_BIG = -1e30
LOG2E = math.log2(math.e)

LANES = 128
HALO = 8
VMEM_LIMIT = 56 * 1024 * 1024


def _params(n_axes, vmem=VMEM_LIMIT):
    return pltpu.CompilerParams(
        dimension_semantics=("arbitrary",) * n_axes, vmem_limit_bytes=vmem)


def _resident(shape):
    nd = len(shape)
    return pl.BlockSpec(shape, lambda *_: (0,) * nd, pipeline_mode=pl.Buffered(1))


def _rms_scale(x):
    return lax.rsqrt(jnp.mean(x * x, axis=-1, keepdims=True) + EPS)


ROW_GROUP = 4
KEY_ROWS = 12
_GROUP_CLASSES = (
    (0, lambda i: 0),
    (-(NA_KH // 2), lambda i: i - NA_KH // 2),
    (ROW_GROUP - KEY_ROWS, lambda i: ROW_GROUP - NA_KH),
)


def _bias_table_kernel(rpb_ref, e_ref):
    h = pl.program_id(0)
    n_dr = 2 * NA_KH - 1
    n_dc = 2 * NA_KW - 1
    cq = lax.broadcasted_iota(jnp.int32, (GRID_W, LANES), 0)
    lane = lax.broadcasted_iota(jnp.int32, (GRID_W, LANES), 1)
    ck = lane % GRID_W
    upper = lane >= GRID_W
    dc = jnp.clip(ck - cq, -(NA_KW - 1), NA_KW - 1) + (NA_KW - 1)
    cs = jnp.clip(cq - NA_KW // 2, 0, GRID_W - NA_KW)
    inside = (ck >= cs) & (ck < cs + NA_KW)
    neg = jnp.full((GRID_W, LANES), NEG_BIG, F32)
    tiles = []
    for dr in range(n_dr):
        acc = neg
        for d in range(n_dc):
            acc = jnp.where(inside & (dc == d), rpb_ref[h * n_dr + dr, d] * LOG2E, acc)
        tiles.append(acc)

    def half(ws_off, rs_off, i, j):
        row = ws_off + j
        if rs_off <= row < rs_off + NA_KH:
            return tiles[row - i + NA_KH - 1]
        return neg

    for cls, (ws_off, rs_fn) in enumerate(_GROUP_CLASSES):
        for i in range(ROW_GROUP):
            for j in range(0, KEY_ROWS, 2):
                lo = half(ws_off, rs_fn(i), i, j)
                hi = half(ws_off, rs_fn(i), i, j + 1)
                e_ref[0, cls, i * GRID_W:(i + 1) * GRID_W, j * GRID_W:(j + 2) * GRID_W] = (
                    jnp.where(upper, hi, lo))


def _bias_table(rpb):
    n_heads = rpb.shape[0]
    rpb2 = rpb.reshape(n_heads * (2 * NA_KH - 1), 2 * NA_KW - 1)
    shape = (n_heads, len(_GROUP_CLASSES), ROW_GROUP * GRID_W, KEY_ROWS * GRID_W)
    return pl.pallas_call(
        _bias_table_kernel,
        grid=(n_heads,),
        in_specs=[pl.BlockSpec(memory_space=pltpu.SMEM)],
        out_specs=pl.BlockSpec((1,) + shape[1:], lambda h: (h, 0, 0, 0)),
        out_shape=jax.ShapeDtypeStruct(shape, F32),
        compiler_params=_params(1),
        name="bias_table",
    )(rpb2)


CHUNK = 512
SLAB = 256


def _inproj_kernel(x_ref, gmix_ref, w_ref, wg_ref, wvt_ref, bgate_ref, gain_ref, bd_ref,
                   proj_ref, gate_ref, vt_ref, *, d_in, normed):
    x = x_ref[...]
    h = (x * _rms_scale(x) * gmix_ref[...]).astype(BF16)
    vt_ref[0] = lax.dot_general(wvt_ref[...], h, (((1,), (1,)), ((), ())),
                                preferred_element_type=F32).astype(BF16)
    for c in range(d_in // CHUNK):
        p = jnp.dot(h, w_ref[:, c * CHUNK:(c + 1) * CHUNK], preferred_element_type=F32)
        if normed[c]:
            for s in range(CHUNK // SLAB):
                col = slice(c * CHUNK + s * SLAB, c * CHUNK + (s + 1) * SLAB)
                ps = p[:, s * SLAB:(s + 1) * SLAB]
                ms = jnp.dot((ps * ps).astype(BF16), bd_ref[...], preferred_element_type=F32)
                proj_ref[:, col] = (ps * lax.rsqrt(ms + EPS) * gain_ref[:, col]).astype(BF16)
        else:
            proj_ref[:, c * CHUNK:(c + 1) * CHUNK] = p.astype(BF16)
    d_gate = gate_ref.shape[1]
    for c in range(d_gate // CHUNK):
        col = slice(c * CHUNK, (c + 1) * CHUNK)
        z = jnp.dot(h, wg_ref[:, col], preferred_element_type=F32) + bgate_ref[:, col]
        gate_ref[:, col] = jax.nn.sigmoid(z).astype(BF16)


def _inproj(x2, gmix, w_in, w_gate, wvt, b_gate, gain, bd, *, d_in, normed, tm):
    t, d = x2.shape
    d_gate = w_gate.shape[1]
    d_v = wvt.shape[0]
    kern = functools.partial(_inproj_kernel, d_in=d_in, normed=normed)
    return pl.pallas_call(
        kern,
        grid=(t // tm,),
        in_specs=[
            pl.BlockSpec((tm, d), lambda i: (i, 0)),
            _resident(gmix.shape),
            _resident(w_in.shape),
            _resident(w_gate.shape),
            _resident(wvt.shape),
            _resident(b_gate.shape),
            _resident(gain.shape),
            _resident(bd.shape),
        ],
        out_specs=[
            pl.BlockSpec((tm, d_in), lambda i: (i, 0)),
            pl.BlockSpec((tm, d_gate), lambda i: (i, 0)),
            pl.BlockSpec((1, d_v, tm), lambda i: (i, 0, 0)),
        ],
        out_shape=[
            jax.ShapeDtypeStruct((t, d_in), BF16),
            jax.ShapeDtypeStruct((t, d_gate), BF16),
            jax.ShapeDtypeStruct((t // tm, d_v, tm), BF16),
        ],
        compiler_params=_params(1),
        name="inproj",
    )(x2, gmix, w_in, w_gate, wvt, b_gate, gain, bd)


def _natten_kernel(flag_ref, shift_ref, q_ref, k_ref, v_ref, e_ref, o_ref, *, groups, n_rows):
    gblk = pl.program_id(2)
    gq = ROW_GROUP * GRID_W
    win = KEY_ROWS * GRID_W
    lane = lax.broadcasted_iota(jnp.int32, (gq, LANES), 1)
    first = lane < HEAD_DIM
    first_k = lax.broadcasted_iota(jnp.int32, (win, LANES), 1) < HEAD_DIM
    ones = jnp.where((lane[:1] % HEAD_DIM == N_POS_FEATS) | (lane[:1] % HEAD_DIM == N_POS_FEATS + 1),
                     1.0, 0.0).astype(BF16)
    def group(gi, exact_max):
        g = gblk * groups + gi
        r0 = g * ROW_GROUP
        ws = jnp.clip(r0 - NA_KH // 2, 0, n_rows - KEY_ROWS)
        cls = jnp.where(r0 == 0, 0, jnp.where(r0 == n_rows - ROW_GROUP, 2, 1))
        q = q_ref[gi * gq:(gi + 1) * gq, :]
        k0 = pl.multiple_of(ws * GRID_W, GRID_W)
        kw = k_ref[pl.ds(k0, win), :]
        vw = v_ref[pl.ds(k0, win), :]
        kones = jnp.broadcast_to(ones, kw.shape)
        ks = (jnp.where(first_k, kw, kones), jnp.where(first_k, kones, kw))
        f = jnp.zeros_like(q) if exact_max else jnp.broadcast_to(shift_ref[0], q.shape)
        qs = (jnp.where(first, q, f), jnp.where(first, f, q))
        outs = []
        for hh in range(2):
            s = lax.dot_general(qs[hh], ks[hh], (((1,), (1,)), ((), ())),
                                preferred_element_type=F32) + e_ref[hh, cls]
            if exact_max:
                s = s - jnp.max(s, axis=-1, keepdims=True)
            p = jnp.exp2(s)
            l = jnp.sum(p, axis=-1, keepdims=True)
            outs.append(jnp.dot(p.astype(BF16), vw, preferred_element_type=F32) / l)
        o_ref[gi * gq:(gi + 1) * gq, :] = jnp.where(first, outs[0], outs[1]).astype(BF16)

    needs_max = flag_ref[0] != 0

    @pl.when(jnp.logical_not(needs_max))
    def _():
        for gi in range(groups):
            group(gi, False)

    @pl.when(needs_max)
    def _():
        for gi in range(groups):
            group(gi, True)


def _natten(proj, e_tab, flag, shifts, *, batch, seq, n_pairs, q_blk, k_blk, v_blk, groups):
    n_rows = seq // GRID_W
    rb = groups * ROW_GROUP
    assert n_rows >= KEY_ROWS and n_rows % rb == 0
    steps = n_rows // rb
    kern = functools.partial(_natten_kernel, groups=groups, n_rows=n_rows)
    return pl.pallas_call(
        kern,
        grid=(n_pairs, batch, steps),
        in_specs=[
            pl.BlockSpec(memory_space=pltpu.SMEM),
            pl.BlockSpec((1, 1, LANES), lambda hp, b, r: (hp, 0, 0)),
            pl.BlockSpec((rb * GRID_W, LANES), lambda hp, b, r: (b * steps + r, q_blk + hp)),
            pl.BlockSpec((seq, LANES), lambda hp, b, r: (b, k_blk + hp)),
            pl.BlockSpec((seq, LANES), lambda hp, b, r: (b, v_blk + hp)),
            pl.BlockSpec((2,) + e_tab.shape[1:], lambda hp, b, r: (hp, 0, 0, 0)),
        ],
        out_specs=pl.BlockSpec((rb * GRID_W, LANES), lambda hp, b, r: (b * steps + r, hp)),
        out_shape=jax.ShapeDtypeStruct((batch * seq, n_pairs * LANES), BF16),
        compiler_params=_params(3),
        name="natten",
    )(flag, shifts, proj, proj, proj, e_tab)


POS_RADIX = 64
N_POS_FEATS = 8
MAX_SHIFT_GAP = 90.0
BOUND_SLACK = 2.0 ** -6


def _split_bf16_np(x):
    hi = np.asarray(x, np.float32).astype(BF16).astype(np.float32)
    return hi, np.float32(x - hi)


def _pos_feats(seq, slopes):
    pos = np.arange(seq)
    hi = (POS_RADIX * (pos // POS_RADIX)).astype(np.float32)
    lo = (pos % POS_RADIX).astype(np.float32)
    one = np.ones(seq, np.float32)
    zero = np.zeros(seq, np.float32)
    qs, ks = [], []
    for slope in np.asarray(slopes, np.float32):
        s_hi, s_lo = (float(v) for v in _split_bf16_np(np.float32(slope)))
        qs.append(np.stack([hi, hi, lo, lo, s_hi * one, s_lo * one, s_hi * one, s_lo * one,
                            zero, zero], axis=1))
        ks.append(np.stack([s_hi * one, s_lo * one, s_hi * one, s_lo * one, -hi, -hi, -lo, -lo,
                            one, one], axis=1))

    def table(feats):
        half = np.zeros((len(feats), seq, HEAD_DIM), np.float32)
        half[:, :, :feats[0].shape[1]] = np.stack(feats)
        return jnp.asarray(np.concatenate([half, half], axis=2), BF16)

    return table(qs), table(ks)


def _split_bf16(x):
    hi = x.astype(BF16).astype(F32)
    return hi, x - hi


def _shift_feats(shift, tq):
    lane = lax.broadcasted_iota(jnp.int32, (tq, LANES), 1) % HEAD_DIM
    hi, lo = _split_bf16(-shift)
    return jnp.where(lane == N_POS_FEATS, hi, jnp.where(lane == N_POS_FEATS + 1, lo, 0.0))


def _shift_bound(gain_q, gain_k, q_scale):
    bound = HEAD_DIM * jnp.max(jnp.abs(gain_q)) * jnp.max(jnp.abs(gain_k)) * q_scale
    return bound * (1.0 + BOUND_SLACK) + BOUND_SLACK


def _shift_table(shifts):
    n = shifts.shape[0]
    hi, lo = _split_bf16(-shifts)
    tab = jnp.zeros((n, 2, HEAD_DIM), F32)
    tab = tab.at[:, :, N_POS_FEATS].set(hi[:, ::-1]).at[:, :, N_POS_FEATS + 1].set(lo[:, ::-1])
    return tab.reshape(n, 1, LANES).astype(BF16)


def _diffattn_kernel(slopes_ref, flag_ref, lamp_ref, dist_ref, shift_ref, qf_ref, kf_ref,
                     q_ref, k_ref, vt_ref, subg_ref, o_ref, m_sc, l_sc, acc_sc,
                     *, tq, tk, seq):
    h = pl.program_id(1)
    qi = pl.program_id(2)
    q = q_ref[...]
    lane = lax.broadcasted_iota(jnp.int32, (tq, LANES), 1)
    first = lane < HEAD_DIM
    n_chunks = seq // tk
    n_slabs = tk // LANES
    posf = qf_ref[0].astype(F32)
    diag_bias = dist_ref[...] * (-slopes_ref[h])

    def operands(feats):
        f = feats.astype(BF16)
        return jnp.where(first, q, f), jnp.where(first, f, q)

    def for_chunks(shiftf, consume, unroll, keys_major):
        q_diag = operands(shiftf)
        q_after = operands(posf + shiftf)
        q_before = operands(shiftf - posf)

        def chunk(c, qs, bias):
            k0 = pl.multiple_of(c * tk, tk)
            kc = k_ref[pl.ds(k0, tk), :]
            kf = kf_ref[0, pl.ds(k0, tk), :]
            ks = (jnp.where(first, kc, kf), jnp.where(first, kf, kc))
            ss = []
            for mi in range(2):
                lhs, rhs = (ks[mi], qs[mi]) if keys_major else (qs[mi], ks[mi])
                s = lax.dot_general(lhs, rhs, (((1,), (1,)), ((), ())),
                                    preferred_element_type=F32)
                ss.append(s if bias is None else s + bias)
            consume(c, ss)

        chunk(qi, q_diag, diag_bias)

        def body(d, carry):
            c = lax.rem(qi + d, n_chunks)
            after = c > qi
            qs = tuple(jnp.where(after, q_after[mi], q_before[mi]) for mi in range(2))
            chunk(c, qs, None)
            return carry

        lax.fori_loop(1, n_chunks, body, 0, unroll=unroll)

    def attend(shiftf, unroll):
        l_sc[...] = jnp.zeros(l_sc.shape, F32)
        acc_sc[...] = jnp.zeros(acc_sc.shape, F32)

        def consume(c, ss):
            vt = vt_ref[c]
            for mi in range(2):
                p = jnp.exp2(ss[mi])
                l_sc[mi] += jnp.sum(p.reshape(tk // HALO, HALO, tq), axis=0)
                acc_sc[mi] += jnp.dot(vt, p.astype(BF16), preferred_element_type=F32)

        for_chunks(shiftf, consume, unroll, True)

    needs_max = flag_ref[0] != 0

    @pl.when(jnp.logical_not(needs_max))
    def _():
        attend(jnp.broadcast_to(shift_ref[0].astype(F32), (tq, LANES)), True)

    @pl.when(needs_max)
    def _():
        m_sc[...] = jnp.full(m_sc.shape, NEG_BIG, F32)

        def consume(c, ss):
            for mi in range(2):
                m = m_sc[mi]
                for j in range(n_slabs):
                    m = jnp.maximum(m, ss[mi][:, j * LANES:(j + 1) * LANES])
                m_sc[mi] = m

        for_chunks(jnp.zeros((tq, LANES), F32), consume, False, False)
        rowmax = [jnp.max(m_sc[mi], axis=-1, keepdims=True) for mi in range(2)]
        attend(_shift_feats(jnp.where(first, rowmax[1], rowmax[0]), tq), False)

    lp = lamp_ref[...]
    lam = (jnp.exp(jnp.sum(lp[0:1] * lp[1:2], axis=-1, keepdims=True))
           - jnp.exp(jnp.sum(lp[2:3] * lp[3:4], axis=-1, keepdims=True)) + LAM_INIT)
    ys = [acc_sc[mi] / jnp.sum(l_sc[mi], axis=0, keepdims=True) for mi in range(2)]
    yt = ys[0] - lam * ys[1]
    yt = yt * lax.rsqrt(jnp.mean(yt * yt, axis=0, keepdims=True) + EPS)
    o_ref[...] = (yt.T * subg_ref[...] * (1.0 - LAM_INIT)).astype(BF16)


def _diffattn(proj, vt, flag, shift, slopes, lam_params, subg, *, batch, seq, n_heads, q_blk,
              k_blk, tq, tk):
    steps = seq // tq
    assert tq == tk and seq % tk == 0 and seq <= POS_RADIX * 256
    assert vt.shape == (batch * seq // tk, n_heads * LANES, tk)
    qfeat, kfeat = _pos_feats(seq, slopes)
    idx = np.arange(tq)
    dist = jnp.asarray(np.abs(idx[:, None] - idx[None, :]), F32)
    kern = functools.partial(_diffattn_kernel, tq=tq, tk=tk, seq=seq)
    return pl.pallas_call(
        kern,
        grid=(batch, n_heads, steps),
        in_specs=[
            pl.BlockSpec(memory_space=pltpu.SMEM),
            pl.BlockSpec(memory_space=pltpu.SMEM),
            pl.BlockSpec(lam_params.shape, lambda b, h, i: (0, 0)),
            _resident(dist.shape),
            _resident(shift.shape),
            pl.BlockSpec((1, tq, LANES), lambda b, h, i: (h, i, 0)),
            pl.BlockSpec((1, seq, LANES), lambda b, h, i: (h, 0, 0)),
            pl.BlockSpec((tq, LANES), lambda b, h, i: (b * steps + i, q_blk + h)),
            pl.BlockSpec((seq, LANES), lambda b, h, i: (b, k_blk + h)),
            pl.BlockSpec((seq // tk, LANES, tk), lambda b, h, i: (b, h, 0)),
            pl.BlockSpec(subg.shape, lambda b, h, i: (0, 0)),
        ],
        out_specs=pl.BlockSpec((tq, LANES), lambda b, h, i: (b * steps + i, h)),
        out_shape=jax.ShapeDtypeStruct((batch * seq, n_heads * LANES), BF16),
        scratch_shapes=[
            pltpu.VMEM((2, tq, LANES), F32),
            pltpu.VMEM((2, HALO, tq), F32),
            pltpu.VMEM((2, LANES, tq), F32),
        ],
        compiler_params=_params(3),
        name="diffattn",
    )(jnp.asarray(slopes, F32), flag, lam_params, dist, shift, qfeat, kfeat,
      proj, proj, vt, subg)


def _mix_kernel(x_ref, ya_ref, yb_ref, g_ref, wa_ref, wb_ref, wo_ref, o_ref):
    d = x_ref.shape[1]
    pa = jnp.dot(ya_ref[...], wa_ref[...], preferred_element_type=F32)
    pb = jnp.dot(yb_ref[...], wb_ref[...], preferred_element_type=F32)
    mixed = g_ref[:, :d].astype(F32) * pa + g_ref[:, d:].astype(F32) * pb
    o_ref[...] = x_ref[...] + jnp.dot(mixed.astype(BF16), wo_ref[...],
                                      preferred_element_type=F32)


def _mix(x2, ya, yb, gate, wa, wb, wo, *, tm):
    t, d = x2.shape
    return pl.pallas_call(
        _mix_kernel,
        grid=(t // tm,),
        in_specs=[
            pl.BlockSpec((tm, d), lambda i: (i, 0)),
            pl.BlockSpec((tm, ya.shape[1]), lambda i: (i, 0)),
            pl.BlockSpec((tm, yb.shape[1]), lambda i: (i, 0)),
            pl.BlockSpec((tm, gate.shape[1]), lambda i: (i, 0)),
            _resident(wa.shape),
            _resident(wb.shape),
            _resident(wo.shape),
        ],
        out_specs=pl.BlockSpec((tm, d), lambda i: (i, 0)),
        out_shape=jax.ShapeDtypeStruct((t, d), F32),
        compiler_params=_params(1),
        name="mix",
    )(x2, ya, yb, gate, wa, wb, wo)


FF_CHUNK = 256
FF_SLOTS = 2


def _ffn_kernel(x_ref, xp_ref, xn_ref, p_ref, gffn_ref, wup_ref, cw_ref, cb_ref, wdn_ref,
                gple_ref, wpg_ref, wpp_ref, o_ref, a_sc, h_sc, *slots, tm, seq, d_ff):
    uv_sc, ug_sc = (slots[i * FF_SLOTS:(i + 1) * FF_SLOTS] for i in range(2))
    i = pl.program_id(0)
    tiles_per_seq = seq // tm
    pos = i % tiles_per_seq
    keep_prev = (pos != 0).astype(F32)
    keep_next = (pos != tiles_per_seq - 1).astype(F32)
    x = x_ref[...]
    g = gffn_ref[...]

    def norm(v):
        return v * _rms_scale(v) * g

    h_sc[...] = jnp.concatenate(
        [norm(xp_ref[...]) * keep_prev, norm(x), norm(xn_ref[...]) * keep_next],
        axis=0).astype(BF16)
    n_chunks = d_ff // FF_CHUNK

    def cols(c):
        return slice(c * FF_CHUNK, (c + 1) * FF_CHUNK)

    def conv(u_ref, col):
        w = cw_ref[:, col]
        u = u_ref[...]
        rows = u.shape[0]
        prev = pltpu.roll(u, 1, 0)[HALO:HALO + tm]
        nxt = pltpu.roll(u, rows - 1, 0)[HALO:HALO + tm]
        return w[0:1] * prev + w[1:2] * u[HALO:HALO + tm] + w[2:3] * nxt + cb_ref[:, col]

    def gelu_tanh(v):
        c = math.sqrt(2.0 / math.pi)
        return v * (0.5 + 0.5 * jnp.tanh(v * (c + (c * 0.044715) * (v * v))))

    def up(c, slot):
        uv_sc[slot][...] = jnp.dot(h_sc[...], wup_ref[:, cols(c)], preferred_element_type=F32)
        ug_sc[slot][...] = jnp.dot(h_sc[...], wup_ref[:, cols(n_chunks + c)],
                                   preferred_element_type=F32)

    def act(c, slot):
        uv = conv(uv_sc[slot], cols(c))
        ug = conv(ug_sc[slot], cols(n_chunks + c))
        a_sc[:, cols(c)] = (gelu_tanh(ug) * uv).astype(BF16)

    up(0, 0)
    for c in range(n_chunks):
        if c + 1 < n_chunks:
            up(c + 1, (c + 1) % FF_SLOTS)
        act(c, c % FF_SLOTS)

    x2 = x + jnp.dot(a_sc[...], wdn_ref[...], preferred_element_type=F32)
    h3 = (x2 * _rms_scale(x2) * gple_ref[...]).astype(BF16)
    pg = jax.nn.sigmoid(jnp.dot(h3, wpg_ref[...], preferred_element_type=F32))
    pp = jnp.dot(p_ref[...].astype(BF16), wpp_ref[...], preferred_element_type=F32)
    o_ref[...] = x2 + pg * pp


def _ffn(x1, p2, gffn, wup, cw, cb, wdn, gple, wpg, wpp, *, seq, tm):
    t, d = x1.shape
    d_ff = wdn.shape[0]
    assert seq % tm == 0 and tm % HALO == 0 and d_ff % FF_CHUNK == 0
    hb = tm // HALO
    last = t // HALO - 1
    kern = functools.partial(_ffn_kernel, tm=tm, seq=seq, d_ff=d_ff)
    return pl.pallas_call(
        kern,
        grid=(t // tm,),
        in_specs=[
            pl.BlockSpec((tm, d), lambda i: (i, 0)),
            pl.BlockSpec((HALO, d), lambda i: (jnp.maximum(i * hb - 1, 0), 0)),
            pl.BlockSpec((HALO, d), lambda i: (jnp.minimum((i + 1) * hb, last), 0)),
            pl.BlockSpec((tm, p2.shape[1]), lambda i: (i, 0)),
            _resident(gffn.shape),
            _resident(wup.shape),
            _resident(cw.shape),
            _resident(cb.shape),
            _resident(wdn.shape),
            _resident(gple.shape),
            _resident(wpg.shape),
            _resident(wpp.shape),
        ],
        out_specs=pl.BlockSpec((tm, d), lambda i: (i, 0)),
        out_shape=jax.ShapeDtypeStruct((t, d), F32),
        scratch_shapes=[
            pltpu.VMEM((tm, d_ff), BF16),
            pltpu.VMEM((tm + 2 * HALO, d), BF16),
        ] + [pltpu.VMEM((tm + 2 * HALO, FF_CHUNK), F32)] * (2 * FF_SLOTS),
        compiler_params=_params(1),
        name="ffn",
    )(x1, x1, x1, p2, gffn, wup, cw, cb, wdn, gple, wpg, wpp)


def kernel(x, p, norm_mix_g, w_in, qn_a_q, qn_a_k, rpb, qn_b_q, qn_b_k, lam_q1, lam_k1, lam_q2, lam_k2, subln_g, w_proj_a, w_proj_b, w_gate, b_gate, w_out, norm_ffn_g, w_up, conv_w, conv_b, w_down, norm_ple_g, w_ple_gate, w_ple_proj):
    batch, seq, d = x.shape
    depth = p.shape[0]
    assert depth == 1
    t = batch * seq
    width_a = w_proj_a.shape[1]
    width_b = w_proj_b.shape[1]
    n_heads_a = width_a // HEAD_DIM
    n_heads_b = width_b // (2 * HEAD_DIM)
    d_in = 3 * width_a + 2 * width_b
    assert width_a == CHUNK and width_b == CHUNK

    scale = 1.0 / math.sqrt(HEAD_DIM)
    ones = jnp.ones((CHUNK,), F32)
    gain = jnp.concatenate([
        jnp.tile(qn_a_q[0], n_heads_a) * (scale * LOG2E), jnp.tile(qn_a_k[0], n_heads_a), ones,
        jnp.tile(qn_b_q[0], 2 * n_heads_b) * (scale * LOG2E), jnp.tile(qn_b_k[0], 2 * n_heads_b),
    ])[None, :]
    normed = (True, True, False, True, True)
    bd = jnp.asarray(np.kron(np.eye(SLAB // HEAD_DIM), np.full((HEAD_DIM, HEAD_DIM), 1.0 / HEAD_DIM)),
                     BF16)
    wvt = w_in[0][:, d_in:].T.astype(BF16)

    x2 = x.reshape(t, d)
    proj, gate, vt = _inproj(x2, norm_mix_g, w_in[0].astype(BF16), w_gate[0].astype(BF16), wvt,
                             b_gate, gain, bd, d_in=d_in, normed=normed, tm=512)

    e_tab = _bias_table(rpb[0])
    blk = CHUNK // LANES
    n_pairs = n_heads_a // 2
    bound_a = _shift_bound(qn_a_q[0], qn_a_k[0], scale * LOG2E)
    b_hi = jnp.max(rpb[0], axis=(1, 2)) * LOG2E
    b_self = rpb[0][:, NA_KH - 1, NA_KW - 1] * LOG2E
    gap_a = 2.0 * bound_a + jnp.max(b_hi - b_self)
    flag_a = jnp.logical_not(gap_a <= MAX_SHIFT_GAP).astype(jnp.int32).reshape(1)
    ya = _natten(proj, e_tab, flag_a, _shift_table((bound_a + b_hi).reshape(n_pairs, 2)),
                 batch=batch, seq=seq, n_pairs=n_pairs, q_blk=0, k_blk=blk, v_blk=2 * blk,
                 groups=4)

    slopes = (LOG2E * 2.0 ** (-8.0 * (np.arange(n_heads_b) + 1.0) / n_heads_b)).astype(np.float32)
    lam_params = jnp.concatenate([lam_q1, lam_k1, lam_q2, lam_k2], axis=0)
    bound_b = _shift_bound(qn_b_q[0], qn_b_k[0], scale * LOG2E)
    flag_b = jnp.logical_not(2.0 * bound_b <= MAX_SHIFT_GAP).astype(jnp.int32).reshape(1)
    shift_b = _shift_table(jnp.broadcast_to(bound_b, (1, 2)))
    yb = _diffattn(proj, vt, flag_b, shift_b, slopes, lam_params, subln_g, batch=batch, seq=seq, n_heads=n_heads_b,
                   q_blk=3 * blk, k_blk=4 * blk, tq=512, tk=512)

    x1 = _mix(x2, ya, yb, gate, w_proj_a[0].astype(BF16), w_proj_b[0].astype(BF16),
              w_out[0].astype(BF16), tm=512)

    out = _ffn(x1, p[0].reshape(t, -1), norm_ffn_g, w_up[0].astype(BF16), conv_w[0], conv_b,
               w_down[0].astype(BF16), norm_ple_g, w_ple_gate[0].astype(BF16),
               w_ple_proj[0].astype(BF16), seq=seq, tm=512)
    return out.reshape(batch, seq, d)
```

```python
import functools
import math

import jax
import jax.numpy as jnp
import numpy as np
from jax import lax
from jax.experimental import pallas as pl
from jax.experimental.pallas import tpu as pltpu

F32 = jnp.float32
BF16 = jnp.bfloat16

HEAD_DIM = 64
GRID_W = 64
NA_KH = 8
NA_KW = 16
CONV_W = 3
EPS = 1e-6
LAM_INIT = 0.8 - 0.6 * math.exp(-0.3 * 0)
NEG_BIG = -1e30
LOG2E = math.log2(math.e)

LANES = 128
HALO = 8
VMEM_LIMIT = 56 * 1024 * 1024


def _params(n_axes, vmem=VMEM_LIMIT):
    return pltpu.CompilerParams(
        dimension_semantics=("arbitrary",) * n_axes, vmem_limit_bytes=vmem)


def _resident(shape):
    nd = len(shape)
    return pl.BlockSpec(shape, lambda *_: (0,) * nd, pipeline_mode=pl.Buffered(1))


def _rms_scale(x):
    return lax.rsqrt(jnp.mean(x * x, axis=-1, keepdims=True) + EPS)


ROW_GROUP = 4
KEY_ROWS = 12
_GROUP_CLASSES = (
    (0, lambda i: 0),
    (-(NA_KH // 2), lambda i: i - NA_KH // 2),
    (ROW_GROUP - KEY_ROWS, lambda i: ROW_GROUP - NA_KH),
)


def _bias_table_kernel(rpb_ref, e_ref):
    h = pl.program_id(0)
    n_dr = 2 * NA_KH - 1
    n_dc = 2 * NA_KW - 1
    cq = lax.broadcasted_iota(jnp.int32, (GRID_W, LANES), 0)
    lane = lax.broadcasted_iota(jnp.int32, (GRID_W, LANES), 1)
    ck = lane % GRID_W
    upper = lane >= GRID_W
    dc = jnp.clip(ck - cq, -(NA_KW - 1), NA_KW - 1) + (NA_KW - 1)
    cs = jnp.clip(cq - NA_KW // 2, 0, GRID_W - NA_KW)
    inside = (ck >= cs) & (ck < cs + NA_KW)
    neg = jnp.full((GRID_W, LANES), NEG_BIG, F32)
    tiles = []
    for dr in range(n_dr):
        acc = neg
        for d in range(n_dc):
            acc = jnp.where(inside & (dc == d), rpb_ref[h * n_dr + dr, d] * LOG2E, acc)
        tiles.append(acc)

    def half(ws_off, rs_off, i, j):
        row = ws_off + j
        if rs_off <= row < rs_off + NA_KH:
            return tiles[row - i + NA_KH - 1]
        return neg

    for cls, (ws_off, rs_fn) in enumerate(_GROUP_CLASSES):
        for i in range(ROW_GROUP):
            for j in range(0, KEY_ROWS, 2):
                lo = half(ws_off, rs_fn(i), i, j)
                hi = half(ws_off, rs_fn(i), i, j + 1)
                e_ref[0, cls, i * GRID_W:(i + 1) * GRID_W, j * GRID_W:(j + 2) * GRID_W] = (
                    jnp.where(upper, hi, lo))


def _bias_table(rpb):
    n_heads = rpb.shape[0]
    rpb2 = rpb.reshape(n_heads * (2 * NA_KH - 1), 2 * NA_KW - 1)
    shape = (n_heads, len(_GROUP_CLASSES), ROW_GROUP * GRID_W, KEY_ROWS * GRID_W)
    return pl.pallas_call(
        _bias_table_kernel,
        grid=(n_heads,),
        in_specs=[pl.BlockSpec(memory_space=pltpu.SMEM)],
        out_specs=pl.BlockSpec((1,) + shape[1:], lambda h: (h, 0, 0, 0)),
        out_shape=jax.ShapeDtypeStruct(shape, F32),
        compiler_params=_params(1),
        name="bias_table",
    )(rpb2)


CHUNK = 512
SLAB = 256


def _inproj_kernel(x_ref, gmix_ref, w_ref, wg_ref, wvt_ref, bgate_ref, gain_ref, bd_ref,
                   proj_ref, gate_ref, vt_ref, *, d_in, normed):
    x = x_ref[...]
    h = (x * _rms_scale(x) * gmix_ref[...]).astype(BF16)
    vt_ref[0] = lax.dot_general(wvt_ref[...], h, (((1,), (1,)), ((), ())),
                                preferred_element_type=F32).astype(BF16)
    for c in range(d_in // CHUNK):
        p = jnp.dot(h, w_ref[:, c * CHUNK:(c + 1) * CHUNK], preferred_element_type=F32)
        if normed[c]:
            for s in range(CHUNK // SLAB):
                col = slice(c * CHUNK + s * SLAB, c * CHUNK + (s + 1) * SLAB)
                ps = p[:, s * SLAB:(s + 1) * SLAB]
                ms = jnp.dot((ps * ps).astype(BF16), bd_ref[...], preferred_element_type=F32)
                proj_ref[:, col] = (ps * lax.rsqrt(ms + EPS) * gain_ref[:, col]).astype(BF16)
        else:
            proj_ref[:, c * CHUNK:(c + 1) * CHUNK] = p.astype(BF16)
    d_gate = gate_ref.shape[1]
    for c in range(d_gate // CHUNK):
        col = slice(c * CHUNK, (c + 1) * CHUNK)
        z = jnp.dot(h, wg_ref[:, col], preferred_element_type=F32) + bgate_ref[:, col]
        gate_ref[:, col] = jax.nn.sigmoid(z).astype(BF16)


def _inproj(x2, gmix, w_in, w_gate, wvt, b_gate, gain, bd, *, d_in, normed, tm):
    t, d = x2.shape
    d_gate = w_gate.shape[1]
    d_v = wvt.shape[0]
    kern = functools.partial(_inproj_kernel, d_in=d_in, normed=normed)
    return pl.pallas_call(
        kern,
        grid=(t // tm,),
        in_specs=[
            pl.BlockSpec((tm, d), lambda i: (i, 0)),
            _resident(gmix.shape),
            _resident(w_in.shape),
            _resident(w_gate.shape),
            _resident(wvt.shape),
            _resident(b_gate.shape),
            _resident(gain.shape),
            _resident(bd.shape),
        ],
        out_specs=[
            pl.BlockSpec((tm, d_in), lambda i: (i, 0)),
            pl.BlockSpec((tm, d_gate), lambda i: (i, 0)),
            pl.BlockSpec((1, d_v, tm), lambda i: (i, 0, 0)),
        ],
        out_shape=[
            jax.ShapeDtypeStruct((t, d_in), BF16),
            jax.ShapeDtypeStruct((t, d_gate), BF16),
            jax.ShapeDtypeStruct((t // tm, d_v, tm), BF16),
        ],
        compiler_params=_params(1),
        name="inproj",
    )(x2, gmix, w_in, w_gate, wvt, b_gate, gain, bd)


def _natten_kernel(flag_ref, shift_ref, q_ref, k_ref, v_ref, e_ref, o_ref, *, groups, n_rows):
    gblk = pl.program_id(2)
    gq = ROW_GROUP * GRID_W
    win = KEY_ROWS * GRID_W
    lane = lax.broadcasted_iota(jnp.int32, (gq, LANES), 1)
    first = lane < HEAD_DIM
    first_k = lax.broadcasted_iota(jnp.int32, (win, LANES), 1) < HEAD_DIM
    ones = jnp.where((lane[:1] % HEAD_DIM == N_POS_FEATS) | (lane[:1] % HEAD_DIM == N_POS_FEATS + 1),
                     1.0, 0.0).astype(BF16)
    def group(gi, exact_max):
        g = gblk * groups + gi
        r0 = g * ROW_GROUP
        ws = jnp.clip(r0 - NA_KH // 2, 0, n_rows - KEY_ROWS)
        cls = jnp.where(r0 == 0, 0, jnp.where(r0 == n_rows - ROW_GROUP, 2, 1))
        q = q_ref[gi * gq:(gi + 1) * gq, :]
        k0 = pl.multiple_of(ws * GRID_W, GRID_W)
        kw = k_ref[pl.ds(k0, win), :]
        vw = v_ref[pl.ds(k0, win), :]
        kones = jnp.broadcast_to(ones, kw.shape)
        ks = (jnp.where(first_k, kw, kones), jnp.where(first_k, kones, kw))
        f = jnp.zeros_like(q) if exact_max else jnp.broadcast_to(shift_ref[0], q.shape)
        qs = (jnp.where(first, q, f), jnp.where(first, f, q))
        outs = []
        for hh in range(2):
            s = lax.dot_general(qs[hh], ks[hh], (((1,), (1,)), ((), ())),
                                preferred_element_type=F32) + e_ref[hh, cls]
            if exact_max:
                s = s - jnp.max(s, axis=-1, keepdims=True)
            p = jnp.exp2(s)
            l = jnp.sum(p, axis=-1, keepdims=True)
            outs.append(jnp.dot(p.astype(BF16), vw, preferred_element_type=F32) / l)
        o_ref[gi * gq:(gi + 1) * gq, :] = jnp.where(first, outs[0], outs[1]).astype(BF16)

    needs_max = flag_ref[0] != 0

    @pl.when(jnp.logical_not(needs_max))
    def _():
        for gi in range(groups):
            group(gi, False)

    @pl.when(needs_max)
    def _():
        for gi in range(groups):
            group(gi, True)


def _natten(proj, e_tab, flag, shifts, *, batch, seq, n_pairs, q_blk, k_blk, v_blk, groups):
    n_rows = seq // GRID_W
    rb = groups * ROW_GROUP
    assert n_rows >= KEY_ROWS and n_rows % rb == 0
    steps = n_rows // rb
    kern = functools.partial(_natten_kernel, groups=groups, n_rows=n_rows)
    return pl.pallas_call(
        kern,
        grid=(n_pairs, batch, steps),
        in_specs=[
            pl.BlockSpec(memory_space=pltpu.SMEM),
            pl.BlockSpec((1, 1, LANES), lambda hp, b, r: (hp, 0, 0)),
            pl.BlockSpec((rb * GRID_W, LANES), lambda hp, b, r: (b * steps + r, q_blk + hp)),
            pl.BlockSpec((seq, LANES), lambda hp, b, r: (b, k_blk + hp)),
            pl.BlockSpec((seq, LANES), lambda hp, b, r: (b, v_blk + hp)),
            pl.BlockSpec((2,) + e_tab.shape[1:], lambda hp, b, r: (hp, 0, 0, 0)),
        ],
        out_specs=pl.BlockSpec((rb * GRID_W, LANES), lambda hp, b, r: (b * steps + r, hp)),
        out_shape=jax.ShapeDtypeStruct((batch * seq, n_pairs * LANES), BF16),
        compiler_params=_params(3),
        name="natten",
    )(flag, shifts, proj, proj, proj, e_tab)


POS_RADIX = 64
N_POS_FEATS = 8
MAX_SHIFT_GAP = 90.0
BOUND_SLACK = 2.0 ** -6


def _split_bf16_np(x):
    hi = np.asarray(x, np.float32).astype(BF16).astype(np.float32)
    return hi, np.float32(x - hi)


def _pos_feats(seq, slopes):
    pos = np.arange(seq)
    hi = (POS_RADIX * (pos // POS_RADIX)).astype(np.float32)
    lo = (pos % POS_RADIX).astype(np.float32)
    one = np.ones(seq, np.float32)
    zero = np.zeros(seq, np.float32)
    qs, ks = [], []
    for slope in np.asarray(slopes, np.float32):
        s_hi, s_lo = (float(v) for v in _split_bf16_np(np.float32(slope)))
        qs.append(np.stack([hi, hi, lo, lo, s_hi * one, s_lo * one, s_hi * one, s_lo * one,
                            zero, zero], axis=1))
        ks.append(np.stack([s_hi * one, s_lo * one, s_hi * one, s_lo * one, -hi, -hi, -lo, -lo,
                            one, one], axis=1))

    def table(feats):
        half = np.zeros((len(feats), seq, HEAD_DIM), np.float32)
        half[:, :, :feats[0].shape[1]] = np.stack(feats)
        return jnp.asarray(np.concatenate([half, half], axis=2), BF16)

    return table(qs), table(ks)


def _split_bf16(x):
    hi = x.astype(BF16).astype(F32)
    return hi, x - hi


def _shift_feats(shift, tq):
    lane = lax.broadcasted_iota(jnp.int32, (tq, LANES), 1) % HEAD_DIM
    hi, lo = _split_bf16(-shift)
    return jnp.where(lane == N_POS_FEATS, hi, jnp.where(lane == N_POS_FEATS + 1, lo, 0.0))


def _shift_bound(gain_q, gain_k, q_scale):
    bound = HEAD_DIM * jnp.max(jnp.abs(gain_q)) * jnp.max(jnp.abs(gain_k)) * q_scale
    return bound * (1.0 + BOUND_SLACK) + BOUND_SLACK


def _shift_table(shifts):
    n = shifts.shape[0]
    hi, lo = _split_bf16(-shifts)
    tab = jnp.zeros((n, 2, HEAD_DIM), F32)
    tab = tab.at[:, :, N_POS_FEATS].set(hi[:, ::-1]).at[:, :, N_POS_FEATS + 1].set(lo[:, ::-1])
    return tab.reshape(n, 1, LANES).astype(BF16)


def _diffattn_kernel(slopes_ref, flag_ref, lamp_ref, dist_ref, shift_ref, qf_ref, kf_ref,
                     q_ref, k_ref, vt_ref, subg_ref, o_ref, m_sc, *scratch,
                     tq, tk, seq, subs):
    l_scs, acc_scs = scratch[:subs], scratch[subs:]
    h = pl.program_id(1)
    lane = lax.broadcasted_iota(jnp.int32, (tq, LANES), 1)
    first = lane < HEAD_DIM
    n_chunks = seq // tk
    n_slabs = tk // LANES
    diag_bias = dist_ref[...] * (-slopes_ref[h])
    lp = lamp_ref[...]
    lam = (jnp.exp(jnp.sum(lp[0:1] * lp[1:2], axis=-1, keepdims=True))
           - jnp.exp(jnp.sum(lp[2:3] * lp[3:4], axis=-1, keepdims=True)) + LAM_INIT)

    def tile(u, exact_max):
        l_sc, acc_sc = l_scs[u], acc_scs[u]
        qi = pl.program_id(2) * subs + u
        rows = slice(u * tq, (u + 1) * tq)
        q = q_ref[rows, :]
        posf = qf_ref[0, rows, :].astype(F32)

        def operands(feats):
            f = feats.astype(BF16)
            return jnp.where(first, q, f), jnp.where(first, f, q)

        def for_chunks(shiftf, consume, unroll, keys_major):
            q_diag = operands(shiftf)
            q_after = operands(posf + shiftf)
            q_before = operands(shiftf - posf)

            def chunk(c, qs, bias):
                k0 = pl.multiple_of(c * tk, tk)
                kc = k_ref[pl.ds(k0, tk), :]
                kf = kf_ref[0, pl.ds(k0, tk), :]
                ks = (jnp.where(first, kc, kf), jnp.where(first, kf, kc))
                ss = []
                for mi in range(2):
                    lhs, rhs = (ks[mi], qs[mi]) if keys_major else (qs[mi], ks[mi])
                    s = lax.dot_general(lhs, rhs, (((1,), (1,)), ((), ())),
                                        preferred_element_type=F32)
                    ss.append(s if bias is None else s + bias)
                consume(c, ss)

            chunk(qi, q_diag, diag_bias)

            def body(d, carry):
                c = lax.rem(qi + d, n_chunks)
                after = c > qi
                qs = tuple(jnp.where(after, q_after[mi], q_before[mi]) for mi in range(2))
                chunk(c, qs, None)
                return carry

            lax.fori_loop(1, n_chunks, body, 0, unroll=unroll)

        def attend(shiftf, unroll):
            l_sc[...] = jnp.zeros(l_sc.shape, F32)
            acc_sc[...] = jnp.zeros(acc_sc.shape, F32)

            def consume(c, ss):
                vt = vt_ref[c]
                for mi in range(2):
                    p = jnp.exp2(ss[mi])
                    l_sc[mi] += jnp.sum(p.reshape(tk // HALO, HALO, tq), axis=0)
                    acc_sc[mi] += jnp.dot(vt, p.astype(BF16), preferred_element_type=F32)

            for_chunks(shiftf, consume, unroll, True)

        if exact_max:
            m_sc[...] = jnp.full(m_sc.shape, NEG_BIG, F32)

            def consume(c, ss):
                for mi in range(2):
                    m = m_sc[mi]
                    for j in range(n_slabs):
                        m = jnp.maximum(m, ss[mi][:, j * LANES:(j + 1) * LANES])
                    m_sc[mi] = m

            for_chunks(jnp.zeros((tq, LANES), F32), consume, False, False)
            rowmax = [jnp.max(m_sc[mi], axis=-1, keepdims=True) for mi in range(2)]
            attend(_shift_feats(jnp.where(first, rowmax[1], rowmax[0]), tq), False)
        else:
            attend(jnp.broadcast_to(shift_ref[0].astype(F32), (tq, LANES)), True)

        ys = [acc_sc[mi] / jnp.sum(l_sc[mi], axis=0, keepdims=True) for mi in range(2)]
        yt = ys[0] - lam * ys[1]
        yt = yt * lax.rsqrt(jnp.mean(yt * yt, axis=0, keepdims=True) + EPS)
        o_ref[rows, :] = (yt.T * subg_ref[...] * (1.0 - LAM_INIT)).astype(BF16)

    needs_max = flag_ref[0] != 0

    @pl.when(jnp.logical_not(needs_max))
    def _():
        for u in range(subs):
            tile(u, False)

    @pl.when(needs_max)
    def _():
        for u in range(subs):
            tile(u, True)


def _diffattn(proj, vt, flag, shift, slopes, lam_params, subg, *, batch, seq, n_heads, q_blk,
              k_blk, tq, tk, subs):
    rows = subs * tq
    steps = seq // rows
    assert tq == tk and seq % rows == 0 and seq <= POS_RADIX * 256
    assert vt.shape == (batch * seq // tk, n_heads * LANES, tk)
    qfeat, kfeat = _pos_feats(seq, slopes)
    idx = np.arange(tq)
    dist = jnp.asarray(np.abs(idx[:, None] - idx[None, :]), F32)
    kern = functools.partial(_diffattn_kernel, tq=tq, tk=tk, seq=seq, subs=subs)
    return pl.pallas_call(
        kern,
        grid=(batch, n_heads, steps),
        in_specs=[
            pl.BlockSpec(memory_space=pltpu.SMEM),
            pl.BlockSpec(memory_space=pltpu.SMEM),
            pl.BlockSpec(lam_params.shape, lambda b, h, i: (0, 0)),
            _resident(dist.shape),
            _resident(shift.shape),
            pl.BlockSpec((1, rows, LANES), lambda b, h, i: (h, i, 0)),
            pl.BlockSpec((1, seq, LANES), lambda b, h, i: (h, 0, 0)),
            pl.BlockSpec((rows, LANES), lambda b, h, i: (b * steps + i, q_blk + h)),
            pl.BlockSpec((seq, LANES), lambda b, h, i: (b, k_blk + h)),
            pl.BlockSpec((seq // tk, LANES, tk), lambda b, h, i: (b, h, 0)),
            pl.BlockSpec(subg.shape, lambda b, h, i: (0, 0)),
        ],
        out_specs=pl.BlockSpec((rows, LANES), lambda b, h, i: (b * steps + i, h)),
        out_shape=jax.ShapeDtypeStruct((batch * seq, n_heads * LANES), BF16),
        scratch_shapes=(
            [pltpu.VMEM((2, tq, LANES), F32)]
            + [pltpu.VMEM((2, HALO, tq), F32)] * subs
            + [pltpu.VMEM((2, LANES, tq), F32)] * subs),
        compiler_params=_params(3),
        name="diffattn",
    )(jnp.asarray(slopes, F32), flag, lam_params, dist, shift, qfeat, kfeat,
      proj, proj, vt, subg)


def _mix_kernel(x_ref, ya_ref, yb_ref, g_ref, wa_ref, wb_ref, wo_ref, o_ref):
    d = x_ref.shape[1]
    pa = jnp.dot(ya_ref[...], wa_ref[...], preferred_element_type=F32)
    pb = jnp.dot(yb_ref[...], wb_ref[...], preferred_element_type=F32)
    mixed = g_ref[:, :d].astype(F32) * pa + g_ref[:, d:].astype(F32) * pb
    o_ref[...] = x_ref[...] + jnp.dot(mixed.astype(BF16), wo_ref[...],
                                      preferred_element_type=F32)


def _mix(x2, ya, yb, gate, wa, wb, wo, *, tm):
    t, d = x2.shape
    return pl.pallas_call(
        _mix_kernel,
        grid=(t // tm,),
        in_specs=[
            pl.BlockSpec((tm, d), lambda i: (i, 0)),
            pl.BlockSpec((tm, ya.shape[1]), lambda i: (i, 0)),
            pl.BlockSpec((tm, yb.shape[1]), lambda i: (i, 0)),
            pl.BlockSpec((tm, gate.shape[1]), lambda i: (i, 0)),
            _resident(wa.shape),
            _resident(wb.shape),
            _resident(wo.shape),
        ],
        out_specs=pl.BlockSpec((tm, d), lambda i: (i, 0)),
        out_shape=jax.ShapeDtypeStruct((t, d), F32),
        compiler_params=_params(1),
        name="mix",
    )(x2, ya, yb, gate, wa, wb, wo)


FF_CHUNK = 256
FF_SLOTS = 2


def _ffn_kernel(x_ref, xp_ref, xn_ref, p_ref, gffn_ref, wup_ref, cw_ref, cb_ref, wdn_ref,
                gple_ref, wpg_ref, wpp_ref, o_ref, a_sc, h_sc, *slots, tm, seq, d_ff):
    uv_sc, ug_sc = (slots[i * FF_SLOTS:(i + 1) * FF_SLOTS] for i in range(2))
    i = pl.program_id(0)
    tiles_per_seq = seq // tm
    pos = i % tiles_per_seq
    keep_prev = (pos != 0).astype(F32)
    keep_next = (pos != tiles_per_seq - 1).astype(F32)
    x = x_ref[...]
    g = gffn_ref[...]

    def norm(v):
        return v * _rms_scale(v) * g

    h_sc[...] = jnp.concatenate(
        [norm(xp_ref[...]) * keep_prev, norm(x), norm(xn_ref[...]) * keep_next],
        axis=0).astype(BF16)
    n_chunks = d_ff // FF_CHUNK

    def cols(c):
        return slice(c * FF_CHUNK, (c + 1) * FF_CHUNK)

    def conv(u_ref, col):
        w = cw_ref[:, col]
        u = u_ref[...]
        rows = u.shape[0]
        prev = pltpu.roll(u, 1, 0)[HALO:HALO + tm]
        nxt = pltpu.roll(u, rows - 1, 0)[HALO:HALO + tm]
        return w[0:1] * prev + w[1:2] * u[HALO:HALO + tm] + w[2:3] * nxt + cb_ref[:, col]

    def gelu_tanh(v):
        c = math.sqrt(2.0 / math.pi)
        return v * (0.5 + 0.5 * jnp.tanh(v * (c + (c * 0.044715) * (v * v))))

    def up(c, slot):
        uv_sc[slot][...] = jnp.dot(h_sc[...], wup_ref[:, cols(c)], preferred_element_type=F32)
        ug_sc[slot][...] = jnp.dot(h_sc[...], wup_ref[:, cols(n_chunks + c)],
                                   preferred_element_type=F32)

    def act(c, slot):
        uv = conv(uv_sc[slot], cols(c))
        ug = conv(ug_sc[slot], cols(n_chunks + c))
        a_sc[:, cols(c)] = (gelu_tanh(ug) * uv).astype(BF16)

    up(0, 0)
    for c in range(n_chunks):
        if c + 1 < n_chunks:
            up(c + 1, (c + 1) % FF_SLOTS)
        act(c, c % FF_SLOTS)

    x2 = x + jnp.dot(a_sc[...], wdn_ref[...], preferred_element_type=F32)
    h3 = (x2 * _rms_scale(x2) * gple_ref[...]).astype(BF16)
    pg = jax.nn.sigmoid(jnp.dot(h3, wpg_ref[...], preferred_element_type=F32))
    pp = jnp.dot(p_ref[...].astype(BF16), wpp_ref[...], preferred_element_type=F32)
    o_ref[...] = x2 + pg * pp


def _ffn(x1, p2, gffn, wup, cw, cb, wdn, gple, wpg, wpp, *, seq, tm):
    t, d = x1.shape
    d_ff = wdn.shape[0]
    assert seq % tm == 0 and tm % HALO == 0 and d_ff % FF_CHUNK == 0
    hb = tm // HALO
    last = t // HALO - 1
    kern = functools.partial(_ffn_kernel, tm=tm, seq=seq, d_ff=d_ff)
    return pl.pallas_call(
        kern,
        grid=(t // tm,),
        in_specs=[
            pl.BlockSpec((tm, d), lambda i: (i, 0)),
            pl.BlockSpec((HALO, d), lambda i: (jnp.maximum(i * hb - 1, 0), 0)),
            pl.BlockSpec((HALO, d), lambda i: (jnp.minimum((i + 1) * hb, last), 0)),
            pl.BlockSpec((tm, p2.shape[1]), lambda i: (i, 0)),
            _resident(gffn.shape),
            _resident(wup.shape),
            _resident(cw.shape),
            _resident(cb.shape),
            _resident(wdn.shape),
            _resident(gple.shape),
            _resident(wpg.shape),
            _resident(wpp.shape),
        ],
        out_specs=pl.BlockSpec((tm, d), lambda i: (i, 0)),
        out_shape=jax.ShapeDtypeStruct((t, d), F32),
        scratch_shapes=[
            pltpu.VMEM((tm, d_ff), BF16),
            pltpu.VMEM((tm + 2 * HALO, d), BF16),
        ] + [pltpu.VMEM((tm + 2 * HALO, FF_CHUNK), F32)] * (2 * FF_SLOTS),
        compiler_params=_params(1),
        name="ffn",
    )(x1, x1, x1, p2, gffn, wup, cw, cb, wdn, gple, wpg, wpp)


def kernel(x, p, norm_mix_g, w_in, qn_a_q, qn_a_k, rpb, qn_b_q, qn_b_k, lam_q1, lam_k1, lam_q2, lam_k2, subln_g, w_proj_a, w_proj_b, w_gate, b_gate, w_out, norm_ffn_g, w_up, conv_w, conv_b, w_down, norm_ple_g, w_ple_gate, w_ple_proj):
    batch, seq, d = x.shape
    depth = p.shape[0]
    assert depth == 1
    t = batch * seq
    width_a = w_proj_a.shape[1]
    width_b = w_proj_b.shape[1]
    n_heads_a = width_a // HEAD_DIM
    n_heads_b = width_b // (2 * HEAD_DIM)
    d_in = 3 * width_a + 2 * width_b
    assert width_a == CHUNK and width_b == CHUNK

    scale = 1.0 / math.sqrt(HEAD_DIM)
    ones = jnp.ones((CHUNK,), F32)
    gain = jnp.concatenate([
        jnp.tile(qn_a_q[0], n_heads_a) * (scale * LOG2E), jnp.tile(qn_a_k[0], n_heads_a), ones,
        jnp.tile(qn_b_q[0], 2 * n_heads_b) * (scale * LOG2E), jnp.tile(qn_b_k[0], 2 * n_heads_b),
    ])[None, :]
    normed = (True, True, False, True, True)
    bd = jnp.asarray(np.kron(np.eye(SLAB // HEAD_DIM), np.full((HEAD_DIM, HEAD_DIM), 1.0 / HEAD_DIM)),
                     BF16)
    wvt = w_in[0][:, d_in:].T.astype(BF16)

    x2 = x.reshape(t, d)
    proj, gate, vt = _inproj(x2, norm_mix_g, w_in[0].astype(BF16), w_gate[0].astype(BF16), wvt,
                             b_gate, gain, bd, d_in=d_in, normed=normed, tm=512)

    e_tab = _bias_table(rpb[0])
    blk = CHUNK // LANES
    n_pairs = n_heads_a // 2
    bound_a = _shift_bound(qn_a_q[0], qn_a_k[0], scale * LOG2E)
    b_hi = jnp.max(rpb[0], axis=(1, 2)) * LOG2E
    b_self = rpb[0][:, NA_KH - 1, NA_KW - 1] * LOG2E
    gap_a = 2.0 * bound_a + jnp.max(b_hi - b_self)
    flag_a = jnp.logical_not(gap_a <= MAX_SHIFT_GAP).astype(jnp.int32).reshape(1)
    ya = _natten(proj, e_tab, flag_a, _shift_table((bound_a + b_hi).reshape(n_pairs, 2)),
                 batch=batch, seq=seq, n_pairs=n_pairs, q_blk=0, k_blk=blk, v_blk=2 * blk,
                 groups=4)

    slopes = (LOG2E * 2.0 ** (-8.0 * (np.arange(n_heads_b) + 1.0) / n_heads_b)).astype(np.float32)
    lam_params = jnp.concatenate([lam_q1, lam_k1, lam_q2, lam_k2], axis=0)
    bound_b = _shift_bound(qn_b_q[0], qn_b_k[0], scale * LOG2E)
    flag_b = jnp.logical_not(2.0 * bound_b <= MAX_SHIFT_GAP).astype(jnp.int32).reshape(1)
    shift_b = _shift_table(jnp.broadcast_to(bound_b, (1, 2)))
    yb = _diffattn(proj, vt, flag_b, shift_b, slopes, lam_params, subln_g, batch=batch, seq=seq, n_heads=n_heads_b,
                   q_blk=3 * blk, k_blk=4 * blk, tq=512, tk=512, subs=2)

    x1 = _mix(x2, ya, yb, gate, w_proj_a[0].astype(BF16), w_proj_b[0].astype(BF16),
              w_out[0].astype(BF16), tm=512)

    out = _ffn(x1, p[0].reshape(t, -1), norm_ffn_g, w_up[0].astype(BF16), conv_w[0], conv_b,
               w_down[0].astype(BF16), norm_ple_g, w_ple_gate[0].astype(BF16),
               w_ple_proj[0].astype(BF16), seq=seq, tm=512)
    return out.reshape(batch, seq, d)
```

```python
import functools
import math

import jax
import jax.numpy as jnp
import numpy as np
from jax import lax
from jax.experimental import pallas as pl
from jax.experimental.pallas import tpu as pltpu

F32 = jnp.float32
BF16 = jnp.bfloat16

HEAD_DIM = 64
GRID_W = 64
NA_KH = 8
NA_KW = 16
CONV_W = 3
EPS = 1e-6
LAM_INIT = 0.8 - 0.6 * math.exp(-0.3 * 0)
NEG_BIG = -1e30
LOG2E = math.log2(math.e)

LANES = 128
HALO = 8
VMEM_LIMIT = 56 * 1024 * 1024


def _params(n_axes, vmem=VMEM_LIMIT):
    return pltpu.CompilerParams(
        dimension_semantics=("arbitrary",) * n_axes, vmem_limit_bytes=vmem)


def _resident(shape):
    nd = len(shape)
    return pl.BlockSpec(shape, lambda *_: (0,) * nd, pipeline_mode=pl.Buffered(1))


def _rms_scale(x):
    return lax.rsqrt(jnp.mean(x * x, axis=-1, keepdims=True) + EPS)


ROW_GROUP = 4
KEY_ROWS = 12
_GROUP_CLASSES = (
    (0, lambda i: 0),
    (-(NA_KH // 2), lambda i: i - NA_KH // 2),
    (ROW_GROUP - KEY_ROWS, lambda i: ROW_GROUP - NA_KH),
)


def _bias_table_kernel(rpb_ref, e_ref):
    h = pl.program_id(0)
    n_dr = 2 * NA_KH - 1
    n_dc = 2 * NA_KW - 1
    cq = lax.broadcasted_iota(jnp.int32, (GRID_W, LANES), 0)
    lane = lax.broadcasted_iota(jnp.int32, (GRID_W, LANES), 1)
    ck = lane % GRID_W
    upper = lane >= GRID_W
    dc = jnp.clip(ck - cq, -(NA_KW - 1), NA_KW - 1) + (NA_KW - 1)
    cs = jnp.clip(cq - NA_KW // 2, 0, GRID_W - NA_KW)
    inside = (ck >= cs) & (ck < cs + NA_KW)
    neg = jnp.full((GRID_W, LANES), NEG_BIG, F32)
    tiles = []
    for dr in range(n_dr):
        acc = neg
        for d in range(n_dc):
            acc = jnp.where(inside & (dc == d), rpb_ref[h * n_dr + dr, d] * LOG2E, acc)
        tiles.append(acc)

    def half(ws_off, rs_off, i, j):
        row = ws_off + j
        if rs_off <= row < rs_off + NA_KH:
            return tiles[row - i + NA_KH - 1]
        return neg

    for cls, (ws_off, rs_fn) in enumerate(_GROUP_CLASSES):
        for i in range(ROW_GROUP):
            for j in range(0, KEY_ROWS, 2):
                lo = half(ws_off, rs_fn(i), i, j)
                hi = half(ws_off, rs_fn(i), i, j + 1)
                e_ref[0, cls, i * GRID_W:(i + 1) * GRID_W, j * GRID_W:(j + 2) * GRID_W] = (
                    jnp.where(upper, hi, lo))


def _bias_table(rpb):
    n_heads = rpb.shape[0]
    rpb2 = rpb.reshape(n_heads * (2 * NA_KH - 1), 2 * NA_KW - 1)
    shape = (n_heads, len(_GROUP_CLASSES), ROW_GROUP * GRID_W, KEY_ROWS * GRID_W)
    return pl.pallas_call(
        _bias_table_kernel,
        grid=(n_heads,),
        in_specs=[pl.BlockSpec(memory_space=pltpu.SMEM)],
        out_specs=pl.BlockSpec((1,) + shape[1:], lambda h: (h, 0, 0, 0)),
        out_shape=jax.ShapeDtypeStruct(shape, F32),
        compiler_params=_params(1),
        name="bias_table",
    )(rpb2)


CHUNK = 512
SLAB = 256


def _inproj_kernel(x_ref, gmix_ref, w_ref, wg_ref, wvt_ref, bgate_ref, gain_ref, bd_ref,
                   proj_ref, gate_ref, vt_ref, *, d_in, normed):
    x = x_ref[...]
    h = (x * _rms_scale(x) * gmix_ref[...]).astype(BF16)
    vt_ref[0] = lax.dot_general(wvt_ref[...], h, (((1,), (1,)), ((), ())),
                                preferred_element_type=F32).astype(BF16)
    for c in range(d_in // CHUNK):
        p = jnp.dot(h, w_ref[:, c * CHUNK:(c + 1) * CHUNK], preferred_element_type=F32)
        if normed[c]:
            for s in range(CHUNK // SLAB):
                col = slice(c * CHUNK + s * SLAB, c * CHUNK + (s + 1) * SLAB)
                ps = p[:, s * SLAB:(s + 1) * SLAB]
                ms = jnp.dot((ps * ps).astype(BF16), bd_ref[...], preferred_element_type=F32)
                proj_ref[:, col] = (ps * lax.rsqrt(ms + EPS) * gain_ref[:, col]).astype(BF16)
        else:
            proj_ref[:, c * CHUNK:(c + 1) * CHUNK] = p.astype(BF16)
    d_gate = gate_ref.shape[1]
    for c in range(d_gate // CHUNK):
        col = slice(c * CHUNK, (c + 1) * CHUNK)
        z = jnp.dot(h, wg_ref[:, col], preferred_element_type=F32) + bgate_ref[:, col]
        gate_ref[:, col] = jax.nn.sigmoid(z).astype(BF16)


def _inproj(x2, gmix, w_in, w_gate, wvt, b_gate, gain, bd, *, d_in, normed, tm):
    t, d = x2.shape
    d_gate = w_gate.shape[1]
    d_v = wvt.shape[0]
    kern = functools.partial(_inproj_kernel, d_in=d_in, normed=normed)
    return pl.pallas_call(
        kern,
        grid=(t // tm,),
        in_specs=[
            pl.BlockSpec((tm, d), lambda i: (i, 0)),
            _resident(gmix.shape),
            _resident(w_in.shape),
            _resident(w_gate.shape),
            _resident(wvt.shape),
            _resident(b_gate.shape),
            _resident(gain.shape),
            _resident(bd.shape),
        ],
        out_specs=[
            pl.BlockSpec((tm, d_in), lambda i: (i, 0)),
            pl.BlockSpec((tm, d_gate), lambda i: (i, 0)),
            pl.BlockSpec((1, d_v, tm), lambda i: (i, 0, 0)),
        ],
        out_shape=[
            jax.ShapeDtypeStruct((t, d_in), BF16),
            jax.ShapeDtypeStruct((t, d_gate), BF16),
            jax.ShapeDtypeStruct((t // tm, d_v, tm), BF16),
        ],
        compiler_params=_params(1),
        name="inproj",
    )(x2, gmix, w_in, w_gate, wvt, b_gate, gain, bd)


def _natten_kernel(flag_ref, shift_ref, q_ref, k_ref, v_ref, e_ref, o_ref, *, groups, n_rows):
    gblk = pl.program_id(2)
    gq = ROW_GROUP * GRID_W
    win = KEY_ROWS * GRID_W
    lane = lax.broadcasted_iota(jnp.int32, (gq, LANES), 1)
    first = lane < HEAD_DIM
    first_k = lax.broadcasted_iota(jnp.int32, (win, LANES), 1) < HEAD_DIM
    ones = jnp.where((lane[:1] % HEAD_DIM == N_POS_FEATS) | (lane[:1] % HEAD_DIM == N_POS_FEATS + 1),
                     1.0, 0.0).astype(BF16)
    def group(gi, exact_max):
        g = gblk * groups + gi
        r0 = g * ROW_GROUP
        ws = jnp.clip(r0 - NA_KH // 2, 0, n_rows - KEY_ROWS)
        cls = jnp.where(r0 == 0, 0, jnp.where(r0 == n_rows - ROW_GROUP, 2, 1))
        q = q_ref[gi * gq:(gi + 1) * gq, :]
        k0 = pl.multiple_of(ws * GRID_W, GRID_W)
        kw = k_ref[pl.ds(k0, win), :]
        vw = v_ref[pl.ds(k0, win), :]
        kones = jnp.broadcast_to(ones, kw.shape)
        ks = (jnp.where(first_k, kw, kones), jnp.where(first_k, kones, kw))
        f = jnp.zeros_like(q) if exact_max else jnp.broadcast_to(shift_ref[0], q.shape)
        qs = (jnp.where(first, q, f), jnp.where(first, f, q))
        outs = []
        for hh in range(2):
            s = lax.dot_general(qs[hh], ks[hh], (((1,), (1,)), ((), ())),
                                preferred_element_type=F32) + e_ref[hh, cls]
            if exact_max:
                s = s - jnp.max(s, axis=-1, keepdims=True)
            p = jnp.exp2(s)
            l = jnp.sum(p, axis=-1, keepdims=True)
            outs.append(jnp.dot(p.astype(BF16), vw, preferred_element_type=F32) / l)
        o_ref[gi * gq:(gi + 1) * gq, :] = jnp.where(first, outs[0], outs[1]).astype(BF16)

    needs_max = flag_ref[0] != 0

    @pl.when(jnp.logical_not(needs_max))
    def _():
        for gi in range(groups):
            group(gi, False)

    @pl.when(needs_max)
    def _():
        for gi in range(groups):
            group(gi, True)


def _natten(proj, e_tab, flag, shifts, *, batch, seq, n_pairs, q_blk, k_blk, v_blk, groups):
    n_rows = seq // GRID_W
    rb = groups * ROW_GROUP
    assert n_rows >= KEY_ROWS and n_rows % rb == 0
    steps = n_rows // rb
    kern = functools.partial(_natten_kernel, groups=groups, n_rows=n_rows)
    return pl.pallas_call(
        kern,
        grid=(n_pairs, batch, steps),
        in_specs=[
            pl.BlockSpec(memory_space=pltpu.SMEM),
            pl.BlockSpec((1, 1, LANES), lambda hp, b, r: (hp, 0, 0)),
            pl.BlockSpec((rb * GRID_W, LANES), lambda hp, b, r: (b * steps + r, q_blk + hp)),
            pl.BlockSpec((seq, LANES), lambda hp, b, r: (b, k_blk + hp)),
            pl.BlockSpec((seq, LANES), lambda hp, b, r: (b, v_blk + hp)),
            pl.BlockSpec((2,) + e_tab.shape[1:], lambda hp, b, r: (hp, 0, 0, 0)),
        ],
        out_specs=pl.BlockSpec((rb * GRID_W, LANES), lambda hp, b, r: (b * steps + r, hp)),
        out_shape=jax.ShapeDtypeStruct((batch * seq, n_pairs * LANES), BF16),
        compiler_params=_params(3),
        name="natten",
    )(flag, shifts, proj, proj, proj, e_tab)


POS_RADIX = 64
N_POS_FEATS = 8
MAX_SHIFT_GAP = 90.0
BOUND_SLACK = 2.0 ** -6


def _split_bf16_np(x):
    hi = np.asarray(x, np.float32).astype(BF16).astype(np.float32)
    return hi, np.float32(x - hi)


def _pos_feats(seq, slopes):
    pos = np.arange(seq)
    hi = (POS_RADIX * (pos // POS_RADIX)).astype(np.float32)
    lo = (pos % POS_RADIX).astype(np.float32)
    one = np.ones(seq, np.float32)
    zero = np.zeros(seq, np.float32)
    qs, ks = [], []
    for slope in np.asarray(slopes, np.float32):
        s_hi, s_lo = (float(v) for v in _split_bf16_np(np.float32(slope)))
        qs.append(np.stack([hi, hi, lo, lo, s_hi * one, s_lo * one, s_hi * one, s_lo * one,
                            zero, zero], axis=1))
        ks.append(np.stack([s_hi * one, s_lo * one, s_hi * one, s_lo * one, -hi, -hi, -lo, -lo,
                            one, one], axis=1))

    def table(feats):
        half = np.zeros((len(feats), seq, HEAD_DIM), np.float32)
        half[:, :, :feats[0].shape[1]] = np.stack(feats)
        return jnp.asarray(np.concatenate([half, half], axis=2), BF16)

    return table(qs), table(ks)


def _split_bf16(x):
    hi = x.astype(BF16).astype(F32)
    return hi, x - hi


def _shift_feats(shift, tq):
    lane = lax.broadcasted_iota(jnp.int32, (tq, LANES), 1) % HEAD_DIM
    hi, lo = _split_bf16(-shift)
    return jnp.where(lane == N_POS_FEATS, hi, jnp.where(lane == N_POS_FEATS + 1, lo, 0.0))


def _shift_bound(gain_q, gain_k, q_scale):
    bound = HEAD_DIM * jnp.max(jnp.abs(gain_q)) * jnp.max(jnp.abs(gain_k)) * q_scale
    return bound * (1.0 + BOUND_SLACK) + BOUND_SLACK


def _shift_table(shifts):
    n = shifts.shape[0]
    hi, lo = _split_bf16(-shifts)
    tab = jnp.zeros((n, 2, HEAD_DIM), F32)
    tab = tab.at[:, :, N_POS_FEATS].set(hi[:, ::-1]).at[:, :, N_POS_FEATS + 1].set(lo[:, ::-1])
    return tab.reshape(n, 1, LANES).astype(BF16)


def _diffattn_kernel(slopes_ref, flag_ref, lamp_ref, dist_ref, shift_ref, qf_ref, kf_ref,
                     q_ref, k_ref, vt_ref, subg_ref, o_ref, m_sc, *scratch,
                     tq, tk, seq, subs):
    l_scs, acc_scs = scratch[:subs], scratch[subs:]
    h = pl.program_id(1)
    lane = lax.broadcasted_iota(jnp.int32, (tq, LANES), 1)
    first = lane < HEAD_DIM
    n_chunks = seq // tk
    n_slabs = tk // LANES
    diag_bias = dist_ref[...] * (-slopes_ref[h])
    lp = lamp_ref[...]
    lam = (jnp.exp(jnp.sum(lp[0:1] * lp[1:2], axis=-1, keepdims=True))
           - jnp.exp(jnp.sum(lp[2:3] * lp[3:4], axis=-1, keepdims=True)) + LAM_INIT)

    def tile(u, exact_max):
        l_sc, acc_sc = l_scs[u], acc_scs[u]
        qi = pl.program_id(2) * subs + u
        rows = slice(u * tq, (u + 1) * tq)
        q = q_ref[rows, :]
        posf = qf_ref[0, rows, :].astype(F32)

        def operands(feats):
            f = feats.astype(BF16)
            return jnp.where(first, q, f), jnp.where(first, f, q)

        def for_chunks(shiftf, consume, unroll, keys_major):
            q_diag = operands(shiftf)
            q_after = operands(posf + shiftf)
            q_before = operands(shiftf - posf)

            def chunk(c, qs, bias):
                k0 = pl.multiple_of(c * tk, tk)
                kc = k_ref[pl.ds(k0, tk), :]
                kf = kf_ref[0, pl.ds(k0, tk), :]
                ks = (jnp.where(first, kc, kf), jnp.where(first, kf, kc))
                ss = []
                for mi in range(2):
                    lhs, rhs = (ks[mi], qs[mi]) if keys_major else (qs[mi], ks[mi])
                    s = lax.dot_general(lhs, rhs, (((1,), (1,)), ((), ())),
                                        preferred_element_type=F32)
                    ss.append(s if bias is None else s + bias)
                consume(c, ss)

            chunk(qi, q_diag, diag_bias)

            def body(d, carry):
                c = lax.rem(qi + d, n_chunks)
                after = c > qi
                qs = tuple(jnp.where(after, q_after[mi], q_before[mi]) for mi in range(2))
                chunk(c, qs, None)
                return carry

            lax.fori_loop(1, n_chunks, body, 0, unroll=unroll)

        def attend(shiftf, unroll):
            l_sc[...] = jnp.zeros(l_sc.shape, F32)
            acc_sc[...] = jnp.zeros(acc_sc.shape, F32)

            def consume(c, ss):
                vt = vt_ref[c]
                for mi in range(2):
                    p = jnp.exp2(ss[mi])
                    l_sc[mi] += jnp.sum(p.reshape(tk // HALO, HALO, tq), axis=0)
                    acc_sc[mi] += jnp.dot(vt, p.astype(BF16), preferred_element_type=F32)

            for_chunks(shiftf, consume, unroll, True)

        if exact_max:
            m_sc[...] = jnp.full(m_sc.shape, NEG_BIG, F32)

            def consume(c, ss):
                for mi in range(2):
                    m = m_sc[mi]
                    for j in range(n_slabs):
                        m = jnp.maximum(m, ss[mi][:, j * LANES:(j + 1) * LANES])
                    m_sc[mi] = m

            for_chunks(jnp.zeros((tq, LANES), F32), consume, False, False)
            rowmax = [jnp.max(m_sc[mi], axis=-1, keepdims=True) for mi in range(2)]
            attend(_shift_feats(jnp.where(first, rowmax[1], rowmax[0]), tq), False)
        else:
            attend(jnp.broadcast_to(shift_ref[0].astype(F32), (tq, LANES)), True)

        ys = [acc_sc[mi] / jnp.sum(l_sc[mi], axis=0, keepdims=True) for mi in range(2)]
        yt = ys[0] - lam * ys[1]
        yt = yt * lax.rsqrt(jnp.mean(yt * yt, axis=0, keepdims=True) + EPS)
        o_ref[rows, :] = (yt.T * subg_ref[...] * (1.0 - LAM_INIT)).astype(BF16)

    needs_max = flag_ref[0] != 0

    @pl.when(jnp.logical_not(needs_max))
    def _():
        for u in range(subs):
            tile(u, False)

    @pl.when(needs_max)
    def _():
        for u in range(subs):
            tile(u, True)


def _diffattn(proj, vt, flag, shift, slopes, lam_params, subg, *, batch, seq, n_heads, q_blk,
              k_blk, tq, tk, subs):
    rows = subs * tq
    steps = seq // rows
    assert tq == tk and seq % rows == 0 and seq <= POS_RADIX * 256
    assert vt.shape == (batch * seq // tk, n_heads * LANES, tk)
    qfeat, kfeat = _pos_feats(seq, slopes)
    idx = np.arange(tq)
    dist = jnp.asarray(np.abs(idx[:, None] - idx[None, :]), F32)
    kern = functools.partial(_diffattn_kernel, tq=tq, tk=tk, seq=seq, subs=subs)
    return pl.pallas_call(
        kern,
        grid=(batch, n_heads, steps),
        in_specs=[
            pl.BlockSpec(memory_space=pltpu.SMEM),
            pl.BlockSpec(memory_space=pltpu.SMEM),
            pl.BlockSpec(lam_params.shape, lambda b, h, i: (0, 0)),
            _resident(dist.shape),
            _resident(shift.shape),
            pl.BlockSpec((1, rows, LANES), lambda b, h, i: (h, i, 0)),
            pl.BlockSpec((1, seq, LANES), lambda b, h, i: (h, 0, 0)),
            pl.BlockSpec((rows, LANES), lambda b, h, i: (b * steps + i, q_blk + h)),
            pl.BlockSpec((seq, LANES), lambda b, h, i: (b, k_blk + h)),
            pl.BlockSpec((seq // tk, LANES, tk), lambda b, h, i: (b, h, 0)),
            pl.BlockSpec(subg.shape, lambda b, h, i: (0, 0)),
        ],
        out_specs=pl.BlockSpec((rows, LANES), lambda b, h, i: (b * steps + i, h)),
        out_shape=jax.ShapeDtypeStruct((batch * seq, n_heads * LANES), BF16),
        scratch_shapes=(
            [pltpu.VMEM((2, tq, LANES), F32)]
            + [pltpu.VMEM((2, HALO, tq), F32)] * subs
            + [pltpu.VMEM((2, LANES, tq), F32)] * subs),
        compiler_params=_params(3),
        name="diffattn",
    )(jnp.asarray(slopes, F32), flag, lam_params, dist, shift, qfeat, kfeat,
      proj, proj, vt, subg)


def _mix_kernel(x_ref, ya_ref, yb_ref, g_ref, wa_ref, wb_ref, wo_ref, o_ref):
    d = x_ref.shape[1]
    pa = jnp.dot(ya_ref[...], wa_ref[...], preferred_element_type=F32)
    pb = jnp.dot(yb_ref[...], wb_ref[...], preferred_element_type=F32)
    mixed = g_ref[:, :d].astype(F32) * pa + g_ref[:, d:].astype(F32) * pb
    o_ref[...] = x_ref[...] + jnp.dot(mixed.astype(BF16), wo_ref[...],
                                      preferred_element_type=F32)


def _mix(x2, ya, yb, gate, wa, wb, wo, *, tm):
    t, d = x2.shape
    return pl.pallas_call(
        _mix_kernel,
        grid=(t // tm,),
        in_specs=[
            pl.BlockSpec((tm, d), lambda i: (i, 0)),
            pl.BlockSpec((tm, ya.shape[1]), lambda i: (i, 0)),
            pl.BlockSpec((tm, yb.shape[1]), lambda i: (i, 0)),
            pl.BlockSpec((tm, gate.shape[1]), lambda i: (i, 0)),
            _resident(wa.shape),
            _resident(wb.shape),
            _resident(wo.shape),
        ],
        out_specs=pl.BlockSpec((tm, d), lambda i: (i, 0)),
        out_shape=jax.ShapeDtypeStruct((t, d), F32),
        compiler_params=_params(1),
        name="mix",
    )(x2, ya, yb, gate, wa, wb, wo)


FF_CHUNK = 256
FF_SLOTS = 2


def _ffn_kernel(x_ref, xp_ref, xn_ref, p_ref, gffn_ref, wup_ref, cw_ref, cb_ref, wdn_ref,
                gple_ref, wpg_ref, wpp_ref, o_ref, *scratch, tm, seq, d_ff, subs):
    per = 2 + 2 * FF_SLOTS
    i = pl.program_id(0)
    steps_per_seq = seq // (subs * tm)
    pos = i % steps_per_seq
    keep_prev = (pos != 0).astype(F32)
    keep_next = (pos != steps_per_seq - 1).astype(F32)
    g = gffn_ref[...]
    n_chunks = d_ff // FF_CHUNK

    def norm(v):
        return v * _rms_scale(v) * g

    def cols(c):
        return slice(c * FF_CHUNK, (c + 1) * FF_CHUNK)

    def gelu_tanh(v):
        c = math.sqrt(2.0 / math.pi)
        return v * (0.5 + 0.5 * jnp.tanh(v * (c + (c * 0.044715) * (v * v))))

    def conv(u_ref, col):
        w = cw_ref[:, col]
        u = u_ref[...]
        rows = u.shape[0]
        prev = pltpu.roll(u, 1, 0)[HALO:HALO + tm]
        nxt = pltpu.roll(u, rows - 1, 0)[HALO:HALO + tm]
        return w[0:1] * prev + w[1:2] * u[HALO:HALO + tm] + w[2:3] * nxt + cb_ref[:, col]

    def hidden(u):
        a_sc, h_sc = scratch[u * per], scratch[u * per + 1]
        uv_sc = scratch[u * per + 2:u * per + 2 + FF_SLOTS]
        ug_sc = scratch[u * per + 2 + FF_SLOTS:(u + 1) * per]
        r0 = u * tm
        before = (norm(xp_ref[...]) * keep_prev if u == 0
                  else norm(x_ref[r0 - HALO:r0, :]))
        after = (norm(xn_ref[...]) * keep_next if u == subs - 1
                 else norm(x_ref[r0 + tm:r0 + tm + HALO, :]))
        h_sc[...] = jnp.concatenate([before, norm(x_ref[r0:r0 + tm, :]), after],
                                    axis=0).astype(BF16)

        def up(c, slot):
            uv_sc[slot][...] = jnp.dot(h_sc[...], wup_ref[:, cols(c)],
                                       preferred_element_type=F32)
            ug_sc[slot][...] = jnp.dot(h_sc[...], wup_ref[:, cols(n_chunks + c)],
                                       preferred_element_type=F32)

        def act(c, slot):
            uv = conv(uv_sc[slot], cols(c))
            ug = conv(ug_sc[slot], cols(n_chunks + c))
            a_sc[:, cols(c)] = (gelu_tanh(ug) * uv).astype(BF16)

        up(0, 0)
        for c in range(n_chunks):
            if c + 1 < n_chunks:
                up(c + 1, (c + 1) % FF_SLOTS)
            act(c, c % FF_SLOTS)

    def finish(u):
        a_sc = scratch[u * per]
        rows = slice(u * tm, (u + 1) * tm)
        x2 = x_ref[rows, :] + jnp.dot(a_sc[...], wdn_ref[...], preferred_element_type=F32)
        h3 = (x2 * _rms_scale(x2) * gple_ref[...]).astype(BF16)
        pg = jax.nn.sigmoid(jnp.dot(h3, wpg_ref[...], preferred_element_type=F32))
        pp = jnp.dot(p_ref[rows, :].astype(BF16), wpp_ref[...], preferred_element_type=F32)
        o_ref[rows, :] = x2 + pg * pp

    for u in range(subs):
        hidden(u)
    for u in range(subs):
        finish(u)


def _ffn(x1, p2, gffn, wup, cw, cb, wdn, gple, wpg, wpp, *, seq, tm, subs):
    t, d = x1.shape
    d_ff = wdn.shape[0]
    rows = subs * tm
    assert seq % rows == 0 and tm % HALO == 0 and d_ff % FF_CHUNK == 0
    hb = rows // HALO
    last = t // HALO - 1
    kern = functools.partial(_ffn_kernel, tm=tm, seq=seq, d_ff=d_ff, subs=subs)
    tile_scratch = ([pltpu.VMEM((tm, d_ff), BF16), pltpu.VMEM((tm + 2 * HALO, d), BF16)]
                    + [pltpu.VMEM((tm + 2 * HALO, FF_CHUNK), F32)] * (2 * FF_SLOTS))
    return pl.pallas_call(
        kern,
        grid=(t // rows,),
        in_specs=[
            pl.BlockSpec((rows, d), lambda i: (i, 0)),
            pl.BlockSpec((HALO, d), lambda i: (jnp.maximum(i * hb - 1, 0), 0)),
            pl.BlockSpec((HALO, d), lambda i: (jnp.minimum((i + 1) * hb, last), 0)),
            pl.BlockSpec((rows, p2.shape[1]), lambda i: (i, 0)),
            _resident(gffn.shape),
            _resident(wup.shape),
            _resident(cw.shape),
            _resident(cb.shape),
            _resident(wdn.shape),
            _resident(gple.shape),
            _resident(wpg.shape),
            _resident(wpp.shape),
        ],
        out_specs=pl.BlockSpec((rows, d), lambda i: (i, 0)),
        out_shape=jax.ShapeDtypeStruct((t, d), F32),
        scratch_shapes=tile_scratch * subs,
        compiler_params=_params(1),
        name="ffn",
    )(x1, x1, x1, p2, gffn, wup, cw, cb, wdn, gple, wpg, wpp)


def kernel(x, p, norm_mix_g, w_in, qn_a_q, qn_a_k, rpb, qn_b_q, qn_b_k, lam_q1, lam_k1, lam_q2, lam_k2, subln_g, w_proj_a, w_proj_b, w_gate, b_gate, w_out, norm_ffn_g, w_up, conv_w, conv_b, w_down, norm_ple_g, w_ple_gate, w_ple_proj):
    batch, seq, d = x.shape
    depth = p.shape[0]
    assert depth == 1
    t = batch * seq
    width_a = w_proj_a.shape[1]
    width_b = w_proj_b.shape[1]
    n_heads_a = width_a // HEAD_DIM
    n_heads_b = width_b // (2 * HEAD_DIM)
    d_in = 3 * width_a + 2 * width_b
    assert width_a == CHUNK and width_b == CHUNK

    scale = 1.0 / math.sqrt(HEAD_DIM)
    ones = jnp.ones((CHUNK,), F32)
    gain = jnp.concatenate([
        jnp.tile(qn_a_q[0], n_heads_a) * (scale * LOG2E), jnp.tile(qn_a_k[0], n_heads_a), ones,
        jnp.tile(qn_b_q[0], 2 * n_heads_b) * (scale * LOG2E), jnp.tile(qn_b_k[0], 2 * n_heads_b),
    ])[None, :]
    normed = (True, True, False, True, True)
    bd = jnp.asarray(np.kron(np.eye(SLAB // HEAD_DIM), np.full((HEAD_DIM, HEAD_DIM), 1.0 / HEAD_DIM)),
                     BF16)
    wvt = w_in[0][:, d_in:].T.astype(BF16)

    x2 = x.reshape(t, d)
    proj, gate, vt = _inproj(x2, norm_mix_g, w_in[0].astype(BF16), w_gate[0].astype(BF16), wvt,
                             b_gate, gain, bd, d_in=d_in, normed=normed, tm=512)

    e_tab = _bias_table(rpb[0])
    blk = CHUNK // LANES
    n_pairs = n_heads_a // 2
    bound_a = _shift_bound(qn_a_q[0], qn_a_k[0], scale * LOG2E)
    b_hi = jnp.max(rpb[0], axis=(1, 2)) * LOG2E
    b_self = rpb[0][:, NA_KH - 1, NA_KW - 1] * LOG2E
    gap_a = 2.0 * bound_a + jnp.max(b_hi - b_self)
    flag_a = jnp.logical_not(gap_a <= MAX_SHIFT_GAP).astype(jnp.int32).reshape(1)
    ya = _natten(proj, e_tab, flag_a, _shift_table((bound_a + b_hi).reshape(n_pairs, 2)),
                 batch=batch, seq=seq, n_pairs=n_pairs, q_blk=0, k_blk=blk, v_blk=2 * blk,
                 groups=4)

    slopes = (LOG2E * 2.0 ** (-8.0 * (np.arange(n_heads_b) + 1.0) / n_heads_b)).astype(np.float32)
    lam_params = jnp.concatenate([lam_q1, lam_k1, lam_q2, lam_k2], axis=0)
    bound_b = _shift_bound(qn_b_q[0], qn_b_k[0], scale * LOG2E)
    flag_b = jnp.logical_not(2.0 * bound_b <= MAX_SHIFT_GAP).astype(jnp.int32).reshape(1)
    shift_b = _shift_table(jnp.broadcast_to(bound_b, (1, 2)))
    yb = _diffattn(proj, vt, flag_b, shift_b, slopes, lam_params, subln_g, batch=batch, seq=seq, n_heads=n_heads_b,
                   q_blk=3 * blk, k_blk=4 * blk, tq=512, tk=512, subs=2)

    x1 = _mix(x2, ya, yb, gate, w_proj_a[0].astype(BF16), w_proj_b[0].astype(BF16),
              w_out[0].astype(BF16), tm=512)

    out = _ffn(x1, p[0].reshape(t, -1), norm_ffn_g, w_up[0].astype(BF16), conv_w[0], conv_b,
               w_down[0].astype(BF16), norm_ple_g, w_ple_gate[0].astype(BF16),
               w_ple_proj[0].astype(BF16), seq=seq, tm=512, subs=2)
    return out.reshape(batch, seq, d)
```

```python
import functools
import math

import jax
import jax.numpy as jnp
import numpy as np
from jax import lax
from jax.experimental import pallas as pl
from jax.experimental.pallas import tpu as pltpu

F32 = jnp.float32
BF16 = jnp.bfloat16

HEAD_DIM = 64
GRID_W = 64
NA_KH = 8
NA_KW = 16
EPS = 1e-6
LAM_INIT = 0.8 - 0.6 * math.exp(-0.3 * 0)
NEG_BIG = -1e30
LOG2E = math.log2(math.e)

LANES = 128
HALO = 8
VMEM_LIMIT = 56 * 1024 * 1024
BF16_EXACT_INT = 256

ROW_TILE = 512
ATT_TILE = 512
TILES_PER_STEP = 2
NATTEN_GROUPS = 8


def _params(n_axes, vmem=VMEM_LIMIT):
    return pltpu.CompilerParams(
        dimension_semantics=("arbitrary",) * n_axes, vmem_limit_bytes=vmem)


def _resident(shape):
    nd = len(shape)
    return pl.BlockSpec(shape, lambda *_: (0,) * nd, pipeline_mode=pl.Buffered(1))


def _rms_scale(x):
    return lax.rsqrt(jnp.mean(x * x, axis=-1, keepdims=True) + EPS)


ROW_GROUP = 4
KEY_ROWS = 12
_GROUP_CLASSES = (
    (0, lambda i: 0),
    (-(NA_KH // 2), lambda i: i - NA_KH // 2),
    (ROW_GROUP - KEY_ROWS, lambda i: ROW_GROUP - NA_KH),
)


def _bias_table_kernel(rpb_ref, e_ref):
    h = pl.program_id(0)
    n_dr = 2 * NA_KH - 1
    n_dc = 2 * NA_KW - 1
    cq = lax.broadcasted_iota(jnp.int32, (GRID_W, LANES), 0)
    lane = lax.broadcasted_iota(jnp.int32, (GRID_W, LANES), 1)
    ck = lane % GRID_W
    upper = lane >= GRID_W
    dc = jnp.clip(ck - cq, -(NA_KW - 1), NA_KW - 1) + (NA_KW - 1)
    cs = jnp.clip(cq - NA_KW // 2, 0, GRID_W - NA_KW)
    inside = (ck >= cs) & (ck < cs + NA_KW)
    neg = jnp.full((GRID_W, LANES), NEG_BIG, F32)
    tiles = []
    for dr in range(n_dr):
        acc = neg
        for d in range(n_dc):
            acc = jnp.where(inside & (dc == d), rpb_ref[h * n_dr + dr, d] * LOG2E, acc)
        tiles.append(acc)

    def half(ws_off, rs_off, i, j):
        row = ws_off + j
        if rs_off <= row < rs_off + NA_KH:
            return tiles[row - i + NA_KH - 1]
        return neg

    for cls, (ws_off, rs_fn) in enumerate(_GROUP_CLASSES):
        for i in range(ROW_GROUP):
            for j in range(0, KEY_ROWS, 2):
                lo = half(ws_off, rs_fn(i), i, j)
                hi = half(ws_off, rs_fn(i), i, j + 1)
                e_ref[0, cls, i * GRID_W:(i + 1) * GRID_W, j * GRID_W:(j + 2) * GRID_W] = (
                    jnp.where(upper, hi, lo))


def _bias_table(rpb):
    n_heads = rpb.shape[0]
    rpb2 = rpb.reshape(n_heads * (2 * NA_KH - 1), 2 * NA_KW - 1)
    shape = (n_heads, len(_GROUP_CLASSES), ROW_GROUP * GRID_W, KEY_ROWS * GRID_W)
    return pl.pallas_call(
        _bias_table_kernel,
        grid=(n_heads,),
        in_specs=[pl.BlockSpec(memory_space=pltpu.SMEM)],
        out_specs=pl.BlockSpec((1,) + shape[1:], lambda h: (h, 0, 0, 0)),
        out_shape=jax.ShapeDtypeStruct(shape, F32),
        compiler_params=_params(1),
        name="bias_table",
    )(rpb2)


CHUNK = 512
SLAB = 256


def _inproj_kernel(x_ref, gmix_ref, w_ref, wg_ref, wvt_ref, bgate_ref, gain_ref, bd_ref,
                   proj_ref, gate_ref, vt_ref, *, d_in, normed):
    x = x_ref[...]
    h = (x * _rms_scale(x) * gmix_ref[...]).astype(BF16)
    vt_ref[0] = lax.dot_general(wvt_ref[...], h, (((1,), (1,)), ((), ())),
                                preferred_element_type=F32).astype(BF16)
    for c in range(d_in // CHUNK):
        p = jnp.dot(h, w_ref[:, c * CHUNK:(c + 1) * CHUNK], preferred_element_type=F32)
        if normed[c]:
            for s in range(CHUNK // SLAB):
                col = slice(c * CHUNK + s * SLAB, c * CHUNK + (s + 1) * SLAB)
                ps = p[:, s * SLAB:(s + 1) * SLAB]
                ms = jnp.dot((ps * ps).astype(BF16), bd_ref[...], preferred_element_type=F32)
                proj_ref[:, col] = (ps * lax.rsqrt(ms + EPS) * gain_ref[:, col]).astype(BF16)
        else:
            proj_ref[:, c * CHUNK:(c + 1) * CHUNK] = p.astype(BF16)
    d_gate = gate_ref.shape[1]
    for c in range(d_gate // CHUNK):
        col = slice(c * CHUNK, (c + 1) * CHUNK)
        z = jnp.dot(h, wg_ref[:, col], preferred_element_type=F32) + bgate_ref[:, col]
        gate_ref[:, col] = jax.nn.sigmoid(z).astype(BF16)


def _inproj(x2, gmix, w_in, w_gate, wvt, b_gate, gain, bd, *, d_in, normed, tm):
    t, d = x2.shape
    d_gate = w_gate.shape[1]
    d_v = wvt.shape[0]
    kern = functools.partial(_inproj_kernel, d_in=d_in, normed=normed)
    return pl.pallas_call(
        kern,
        grid=(t // tm,),
        in_specs=[
            pl.BlockSpec((tm, d), lambda i: (i, 0)),
            _resident(gmix.shape),
            _resident(w_in.shape),
            _resident(w_gate.shape),
            _resident(wvt.shape),
            _resident(b_gate.shape),
            _resident(gain.shape),
            _resident(bd.shape),
        ],
        out_specs=[
            pl.BlockSpec((tm, d_in), lambda i: (i, 0)),
            pl.BlockSpec((tm, d_gate), lambda i: (i, 0)),
            pl.BlockSpec((1, d_v, tm), lambda i: (i, 0, 0)),
        ],
        out_shape=[
            jax.ShapeDtypeStruct((t, d_in), BF16),
            jax.ShapeDtypeStruct((t, d_gate), BF16),
            jax.ShapeDtypeStruct((t // tm, d_v, tm), BF16),
        ],
        compiler_params=_params(1),
        name="inproj",
    )(x2, gmix, w_in, w_gate, wvt, b_gate, gain, bd)


def _natten_kernel(flag_ref, shift_ref, q_ref, k_ref, v_ref, e_ref, o_ref, *, groups, n_rows):
    gblk = pl.program_id(2)
    gq = ROW_GROUP * GRID_W
    win = KEY_ROWS * GRID_W
    lane = lax.broadcasted_iota(jnp.int32, (gq, LANES), 1)
    first = lane < HEAD_DIM
    first_k = lax.broadcasted_iota(jnp.int32, (win, LANES), 1) < HEAD_DIM
    ones = jnp.where((lane[:1] % HEAD_DIM == N_POS_FEATS) | (lane[:1] % HEAD_DIM == N_POS_FEATS + 1),
                     1.0, 0.0).astype(BF16)
    def group(gi, exact_max):
        g = gblk * groups + gi
        r0 = g * ROW_GROUP
        ws = jnp.clip(r0 - NA_KH // 2, 0, n_rows - KEY_ROWS)
        cls = jnp.where(r0 == 0, 0, jnp.where(r0 == n_rows - ROW_GROUP, 2, 1))
        q = q_ref[gi * gq:(gi + 1) * gq, :]
        k0 = pl.multiple_of(ws * GRID_W, GRID_W)
        kw = k_ref[pl.ds(k0, win), :]
        vw = v_ref[pl.ds(k0, win), :]
        kones = jnp.broadcast_to(ones, kw.shape)
        ks = (jnp.where(first_k, kw, kones), jnp.where(first_k, kones, kw))
        f = jnp.zeros_like(q) if exact_max else jnp.broadcast_to(shift_ref[0], q.shape)
        qs = (jnp.where(first, q, f), jnp.where(first, f, q))
        outs = []
        for hh in range(2):
            s = lax.dot_general(qs[hh], ks[hh], (((1,), (1,)), ((), ())),
                                preferred_element_type=F32) + e_ref[hh, cls]
            if exact_max:
                s = s - jnp.max(s, axis=-1, keepdims=True)
            p = jnp.exp2(s)
            l = jnp.sum(p, axis=-1, keepdims=True)
            outs.append(jnp.dot(p.astype(BF16), vw, preferred_element_type=F32) / l)
        o_ref[gi * gq:(gi + 1) * gq, :] = jnp.where(first, outs[0], outs[1]).astype(BF16)

    needs_max = flag_ref[0] != 0

    @pl.when(jnp.logical_not(needs_max))
    def _():
        for gi in range(groups):
            group(gi, False)

    @pl.when(needs_max)
    def _():
        for gi in range(groups):
            group(gi, True)


def _natten(proj, e_tab, flag, shifts, *, batch, seq, n_pairs, q_blk, k_blk, v_blk, groups):
    n_rows = seq // GRID_W
    rb = groups * ROW_GROUP
    assert n_rows >= KEY_ROWS and n_rows % rb == 0
    steps = n_rows // rb
    kern = functools.partial(_natten_kernel, groups=groups, n_rows=n_rows)
    return pl.pallas_call(
        kern,
        grid=(n_pairs, batch, steps),
        in_specs=[
            pl.BlockSpec(memory_space=pltpu.SMEM),
            pl.BlockSpec((1, 1, LANES), lambda hp, b, r: (hp, 0, 0)),
            pl.BlockSpec((rb * GRID_W, LANES), lambda hp, b, r: (b * steps + r, q_blk + hp)),
            pl.BlockSpec((seq, LANES), lambda hp, b, r: (b, k_blk + hp)),
            pl.BlockSpec((seq, LANES), lambda hp, b, r: (b, v_blk + hp)),
            pl.BlockSpec((2,) + e_tab.shape[1:], lambda hp, b, r: (hp, 0, 0, 0)),
        ],
        out_specs=pl.BlockSpec((rb * GRID_W, LANES), lambda hp, b, r: (b * steps + r, hp)),
        out_shape=jax.ShapeDtypeStruct((batch * seq, n_pairs * LANES), BF16),
        compiler_params=_params(3),
        name="natten",
    )(flag, shifts, proj, proj, proj, e_tab)


POS_RADIX = 64
N_POS_FEATS = 8
MAX_SHIFT_GAP = 90.0
BOUND_SLACK = 2.0 ** -6


def _split_bf16_np(x):
    hi = np.asarray(x, np.float32).astype(BF16).astype(np.float32)
    return hi, np.float32(x - hi)


def _pos_feats(seq, slopes):
    pos = np.arange(seq)
    hi = (POS_RADIX * (pos // POS_RADIX)).astype(np.float32)
    lo = (pos % POS_RADIX).astype(np.float32)
    one = np.ones(seq, np.float32)
    zero = np.zeros(seq, np.float32)
    qs, ks = [], []
    for slope in np.asarray(slopes, np.float32):
        s_hi, s_lo = (float(v) for v in _split_bf16_np(np.float32(slope)))
        qs.append(np.stack([hi, hi, lo, lo, s_hi * one, s_lo * one, s_hi * one, s_lo * one,
                            zero, zero], axis=1))
        ks.append(np.stack([s_hi * one, s_lo * one, s_hi * one, s_lo * one, -hi, -hi, -lo, -lo,
                            one, one], axis=1))

    def table(feats):
        half = np.zeros((len(feats), seq, HEAD_DIM), np.float32)
        half[:, :, :feats[0].shape[1]] = np.stack(feats)
        return jnp.asarray(np.concatenate([half, half], axis=2), BF16)

    return table(qs), table(ks)


def _split_bf16(x):
    hi = x.astype(BF16).astype(F32)
    return hi, x - hi


def _shift_feats(shift, tq):
    lane = lax.broadcasted_iota(jnp.int32, (tq, LANES), 1) % HEAD_DIM
    hi, lo = _split_bf16(-shift)
    return jnp.where(lane == N_POS_FEATS, hi, jnp.where(lane == N_POS_FEATS + 1, lo, 0.0))


def _shift_bound(gain_q, gain_k, q_scale):
    bound = HEAD_DIM * jnp.max(jnp.abs(gain_q)) * jnp.max(jnp.abs(gain_k)) * q_scale
    return bound * (1.0 + BOUND_SLACK) + BOUND_SLACK


def _shift_table(shifts):
    n = shifts.shape[0]
    hi, lo = _split_bf16(-shifts)
    tab = jnp.zeros((n, 2, HEAD_DIM), F32)
    tab = tab.at[:, :, N_POS_FEATS].set(hi[:, ::-1]).at[:, :, N_POS_FEATS + 1].set(lo[:, ::-1])
    return tab.reshape(n, 1, LANES).astype(BF16)


def _diffattn_kernel(slopes_ref, flag_ref, lamp_ref, dist_ref, shift_ref, qf_ref, kf_ref,
                     q_ref, k_ref, vt_ref, subg_ref, o_ref, m_sc, *scratch,
                     tq, tk, seq, subs):
    l_scs, acc_scs = scratch[:subs], scratch[subs:]
    h = pl.program_id(1)
    lane = lax.broadcasted_iota(jnp.int32, (tq, LANES), 1)
    first = lane < HEAD_DIM
    n_chunks = seq // tk
    n_slabs = tk // LANES
    diag_bias = dist_ref[...] * (-slopes_ref[h])
    lp = lamp_ref[...]
    lam = (jnp.exp(jnp.sum(lp[0:1] * lp[1:2], axis=-1, keepdims=True))
           - jnp.exp(jnp.sum(lp[2:3] * lp[3:4], axis=-1, keepdims=True)) + LAM_INIT)

    def tile(u, exact_max):
        l_sc, acc_sc = l_scs[u], acc_scs[u]
        qi = pl.program_id(2) * subs + u
        rows = slice(u * tq, (u + 1) * tq)
        q = q_ref[rows, :]
        posf = qf_ref[0, rows, :].astype(F32)

        def operands(feats):
            f = feats.astype(BF16)
            return jnp.where(first, q, f), jnp.where(first, f, q)

        def for_chunks(shiftf, consume, unroll, keys_major):
            q_diag = operands(shiftf)
            q_after = operands(posf + shiftf)
            q_before = operands(shiftf - posf)

            def chunk(c, qs, bias):
                k0 = pl.multiple_of(c * tk, tk)
                kc = k_ref[pl.ds(k0, tk), :]
                kf = kf_ref[0, pl.ds(k0, tk), :]
                ks = (jnp.where(first, kc, kf), jnp.where(first, kf, kc))
                ss = []
                for mi in range(2):
                    lhs, rhs = (ks[mi], qs[mi]) if keys_major else (qs[mi], ks[mi])
                    s = lax.dot_general(lhs, rhs, (((1,), (1,)), ((), ())),
                                        preferred_element_type=F32)
                    ss.append(s if bias is None else s + bias)
                consume(c, ss)

            chunk(qi, q_diag, diag_bias)

            def body(d, carry):
                c = lax.rem(qi + d, n_chunks)
                after = c > qi
                qs = tuple(jnp.where(after, q_after[mi], q_before[mi]) for mi in range(2))
                chunk(c, qs, None)
                return carry

            lax.fori_loop(1, n_chunks, body, 0, unroll=unroll)

        def attend(shiftf, unroll):
            l_sc[...] = jnp.zeros(l_sc.shape, F32)
            acc_sc[...] = jnp.zeros(acc_sc.shape, F32)

            def consume(c, ss):
                vt = vt_ref[c]
                for mi in range(2):
                    p = jnp.exp2(ss[mi])
                    l_sc[mi] += jnp.sum(p.reshape(tk // HALO, HALO, tq), axis=0)
                    acc_sc[mi] += jnp.dot(vt, p.astype(BF16), preferred_element_type=F32)

            for_chunks(shiftf, consume, unroll, True)

        if exact_max:
            m_sc[...] = jnp.full(m_sc.shape, NEG_BIG, F32)

            def consume(c, ss):
                for mi in range(2):
                    m = m_sc[mi]
                    for j in range(n_slabs):
                        m = jnp.maximum(m, ss[mi][:, j * LANES:(j + 1) * LANES])
                    m_sc[mi] = m

            for_chunks(jnp.zeros((tq, LANES), F32), consume, False, False)
            rowmax = [jnp.max(m_sc[mi], axis=-1, keepdims=True) for mi in range(2)]
            attend(_shift_feats(jnp.where(first, rowmax[1], rowmax[0]), tq), False)
        else:
            attend(jnp.broadcast_to(shift_ref[0].astype(F32), (tq, LANES)), True)

        ys = [acc_sc[mi] / jnp.sum(l_sc[mi], axis=0, keepdims=True) for mi in range(2)]
        yt = ys[0] - lam * ys[1]
        yt = yt * lax.rsqrt(jnp.mean(yt * yt, axis=0, keepdims=True) + EPS)
        o_ref[rows, :] = (yt.T * subg_ref[...] * (1.0 - LAM_INIT)).astype(BF16)

    needs_max = flag_ref[0] != 0

    @pl.when(jnp.logical_not(needs_max))
    def _():
        for u in range(subs):
            tile(u, False)

    @pl.when(needs_max)
    def _():
        for u in range(subs):
            tile(u, True)


def _diffattn(proj, vt, flag, shift, slopes, lam_params, subg, *, batch, seq, n_heads, q_blk,
              k_blk, tq, tk, subs):
    rows = subs * tq
    steps = seq // rows
    assert tq == tk and seq % rows == 0 and seq <= POS_RADIX * BF16_EXACT_INT
    assert vt.shape == (batch * seq // tk, n_heads * LANES, tk)
    qfeat, kfeat = _pos_feats(seq, slopes)
    idx = np.arange(tq)
    dist = jnp.asarray(np.abs(idx[:, None] - idx[None, :]), F32)
    kern = functools.partial(_diffattn_kernel, tq=tq, tk=tk, seq=seq, subs=subs)
    return pl.pallas_call(
        kern,
        grid=(batch, n_heads, steps),
        in_specs=[
            pl.BlockSpec(memory_space=pltpu.SMEM),
            pl.BlockSpec(memory_space=pltpu.SMEM),
            pl.BlockSpec(lam_params.shape, lambda b, h, i: (0, 0)),
            _resident(dist.shape),
            _resident(shift.shape),
            pl.BlockSpec((1, rows, LANES), lambda b, h, i: (h, i, 0)),
            pl.BlockSpec((1, seq, LANES), lambda b, h, i: (h, 0, 0)),
            pl.BlockSpec((rows, LANES), lambda b, h, i: (b * steps + i, q_blk + h)),
            pl.BlockSpec((seq, LANES), lambda b, h, i: (b, k_blk + h)),
            pl.BlockSpec((seq // tk, LANES, tk), lambda b, h, i: (b, h, 0)),
            pl.BlockSpec(subg.shape, lambda b, h, i: (0, 0)),
        ],
        out_specs=pl.BlockSpec((rows, LANES), lambda b, h, i: (b * steps + i, h)),
        out_shape=jax.ShapeDtypeStruct((batch * seq, n_heads * LANES), BF16),
        scratch_shapes=(
            [pltpu.VMEM((2, tq, LANES), F32)]
            + [pltpu.VMEM((2, HALO, tq), F32)] * subs
            + [pltpu.VMEM((2, LANES, tq), F32)] * subs),
        compiler_params=_params(3),
        name="diffattn",
    )(jnp.asarray(slopes, F32), flag, lam_params, dist, shift, qfeat, kfeat,
      proj, proj, vt, subg)


def _mix_kernel(x_ref, ya_ref, yb_ref, g_ref, wa_ref, wb_ref, wo_ref, o_ref):
    d = x_ref.shape[1]
    pa = jnp.dot(ya_ref[...], wa_ref[...], preferred_element_type=F32)
    pb = jnp.dot(yb_ref[...], wb_ref[...], preferred_element_type=F32)
    mixed = g_ref[:, :d].astype(F32) * pa + g_ref[:, d:].astype(F32) * pb
    o_ref[...] = x_ref[...] + jnp.dot(mixed.astype(BF16), wo_ref[...],
                                      preferred_element_type=F32)


def _mix(x2, ya, yb, gate, wa, wb, wo, *, tm):
    t, d = x2.shape
    return pl.pallas_call(
        _mix_kernel,
        grid=(t // tm,),
        in_specs=[
            pl.BlockSpec((tm, d), lambda i: (i, 0)),
            pl.BlockSpec((tm, ya.shape[1]), lambda i: (i, 0)),
            pl.BlockSpec((tm, yb.shape[1]), lambda i: (i, 0)),
            pl.BlockSpec((tm, gate.shape[1]), lambda i: (i, 0)),
            _resident(wa.shape),
            _resident(wb.shape),
            _resident(wo.shape),
        ],
        out_specs=pl.BlockSpec((tm, d), lambda i: (i, 0)),
        out_shape=jax.ShapeDtypeStruct((t, d), F32),
        compiler_params=_params(1),
        name="mix",
    )(x2, ya, yb, gate, wa, wb, wo)


FF_CHUNK = 256
FF_SLOTS = 2


def _ffn_kernel(x_ref, xp_ref, xn_ref, p_ref, gffn_ref, wup_ref, cw_ref, cb_ref, wdn_ref,
                gple_ref, wpg_ref, wpp_ref, o_ref, *scratch, tm, seq, d_ff, subs):
    per = 2 + 2 * FF_SLOTS
    i = pl.program_id(0)
    steps_per_seq = seq // (subs * tm)
    pos = i % steps_per_seq
    keep_prev = (pos != 0).astype(F32)
    keep_next = (pos != steps_per_seq - 1).astype(F32)
    g = gffn_ref[...]
    n_chunks = d_ff // FF_CHUNK

    def norm(v):
        return v * _rms_scale(v) * g

    def cols(c):
        return slice(c * FF_CHUNK, (c + 1) * FF_CHUNK)

    def gelu_tanh(v):
        c = math.sqrt(2.0 / math.pi)
        return v * (0.5 + 0.5 * jnp.tanh(v * (c + (c * 0.044715) * (v * v))))

    def conv(u_ref, col):
        w = cw_ref[:, col]
        u = u_ref[...]
        rows = u.shape[0]
        prev = pltpu.roll(u, 1, 0)[HALO:HALO + tm]
        nxt = pltpu.roll(u, rows - 1, 0)[HALO:HALO + tm]
        return w[0:1] * prev + w[1:2] * u[HALO:HALO + tm] + w[2:3] * nxt + cb_ref[:, col]

    def hidden(u):
        a_sc, h_sc = scratch[u * per], scratch[u * per + 1]
        uv_sc = scratch[u * per + 2:u * per + 2 + FF_SLOTS]
        ug_sc = scratch[u * per + 2 + FF_SLOTS:(u + 1) * per]
        r0 = u * tm
        before = (norm(xp_ref[...]) * keep_prev if u == 0
                  else norm(x_ref[r0 - HALO:r0, :]))
        after = (norm(xn_ref[...]) * keep_next if u == subs - 1
                 else norm(x_ref[r0 + tm:r0 + tm + HALO, :]))
        h_sc[...] = jnp.concatenate([before, norm(x_ref[r0:r0 + tm, :]), after],
                                    axis=0).astype(BF16)

        def up(c, slot):
            uv_sc[slot][...] = jnp.dot(h_sc[...], wup_ref[:, cols(c)],
                                       preferred_element_type=F32)
            ug_sc[slot][...] = jnp.dot(h_sc[...], wup_ref[:, cols(n_chunks + c)],
                                       preferred_element_type=F32)

        def act(c, slot):
            uv = conv(uv_sc[slot], cols(c))
            ug = conv(ug_sc[slot], cols(n_chunks + c))
            a_sc[:, cols(c)] = (gelu_tanh(ug) * uv).astype(BF16)

        up(0, 0)
        for c in range(n_chunks):
            if c + 1 < n_chunks:
                up(c + 1, (c + 1) % FF_SLOTS)
            act(c, c % FF_SLOTS)

    def finish(u):
        a_sc = scratch[u * per]
        rows = slice(u * tm, (u + 1) * tm)
        x2 = x_ref[rows, :] + jnp.dot(a_sc[...], wdn_ref[...], preferred_element_type=F32)
        h3 = (x2 * _rms_scale(x2) * gple_ref[...]).astype(BF16)
        pg = jax.nn.sigmoid(jnp.dot(h3, wpg_ref[...], preferred_element_type=F32))
        pp = jnp.dot(p_ref[rows, :].astype(BF16), wpp_ref[...], preferred_element_type=F32)
        o_ref[rows, :] = x2 + pg * pp

    for u in range(subs):
        hidden(u)
    for u in range(subs):
        finish(u)


def _ffn(x1, p2, gffn, wup, cw, cb, wdn, gple, wpg, wpp, *, seq, tm, subs):
    t, d = x1.shape
    d_ff = wdn.shape[0]
    rows = subs * tm
    assert seq % rows == 0 and tm % HALO == 0 and d_ff % FF_CHUNK == 0
    hb = rows // HALO
    last = t // HALO - 1
    kern = functools.partial(_ffn_kernel, tm=tm, seq=seq, d_ff=d_ff, subs=subs)
    tile_scratch = ([pltpu.VMEM((tm, d_ff), BF16), pltpu.VMEM((tm + 2 * HALO, d), BF16)]
                    + [pltpu.VMEM((tm + 2 * HALO, FF_CHUNK), F32)] * (2 * FF_SLOTS))
    return pl.pallas_call(
        kern,
        grid=(t // rows,),
        in_specs=[
            pl.BlockSpec((rows, d), lambda i: (i, 0)),
            pl.BlockSpec((HALO, d), lambda i: (jnp.maximum(i * hb - 1, 0), 0)),
            pl.BlockSpec((HALO, d), lambda i: (jnp.minimum((i + 1) * hb, last), 0)),
            pl.BlockSpec((rows, p2.shape[1]), lambda i: (i, 0)),
            _resident(gffn.shape),
            _resident(wup.shape),
            _resident(cw.shape),
            _resident(cb.shape),
            _resident(wdn.shape),
            _resident(gple.shape),
            _resident(wpg.shape),
            _resident(wpp.shape),
        ],
        out_specs=pl.BlockSpec((rows, d), lambda i: (i, 0)),
        out_shape=jax.ShapeDtypeStruct((t, d), F32),
        scratch_shapes=tile_scratch * subs,
        compiler_params=_params(1),
        name="ffn",
    )(x1, x1, x1, p2, gffn, wup, cw, cb, wdn, gple, wpg, wpp)


def kernel(x, p, norm_mix_g, w_in, qn_a_q, qn_a_k, rpb, qn_b_q, qn_b_k, lam_q1, lam_k1, lam_q2, lam_k2, subln_g, w_proj_a, w_proj_b, w_gate, b_gate, w_out, norm_ffn_g, w_up, conv_w, conv_b, w_down, norm_ple_g, w_ple_gate, w_ple_proj):
    batch, seq, d = x.shape
    depth = p.shape[0]
    assert depth == 1
    t = batch * seq
    width_a = w_proj_a.shape[1]
    width_b = w_proj_b.shape[1]
    n_heads_a = width_a // HEAD_DIM
    n_heads_b = width_b // (2 * HEAD_DIM)
    d_in = 3 * width_a + 2 * width_b
    assert width_a == CHUNK and width_b == CHUNK

    scale = 1.0 / math.sqrt(HEAD_DIM)
    ones = jnp.ones((CHUNK,), F32)
    gain = jnp.concatenate([
        jnp.tile(qn_a_q[0], n_heads_a) * (scale * LOG2E), jnp.tile(qn_a_k[0], n_heads_a), ones,
        jnp.tile(qn_b_q[0], 2 * n_heads_b) * (scale * LOG2E), jnp.tile(qn_b_k[0], 2 * n_heads_b),
    ])[None, :]
    normed = (True, True, False, True, True)
    bd = jnp.asarray(np.kron(np.eye(SLAB // HEAD_DIM), np.full((HEAD_DIM, HEAD_DIM), 1.0 / HEAD_DIM)),
                     BF16)
    wvt = w_in[0][:, d_in:].T.astype(BF16)

    x2 = x.reshape(t, d)
    proj, gate, vt = _inproj(x2, norm_mix_g, w_in[0].astype(BF16), w_gate[0].astype(BF16), wvt,
                             b_gate, gain, bd, d_in=d_in, normed=normed, tm=ROW_TILE)

    e_tab = _bias_table(rpb[0])
    blk = CHUNK // LANES
    n_pairs = n_heads_a // 2
    bound_a = _shift_bound(qn_a_q[0], qn_a_k[0], scale * LOG2E)
    b_hi = jnp.max(rpb[0], axis=(1, 2)) * LOG2E
    b_self = rpb[0][:, NA_KH - 1, NA_KW - 1] * LOG2E
    gap_a = 2.0 * bound_a + jnp.max(b_hi - b_self)
    flag_a = jnp.logical_not(gap_a <= MAX_SHIFT_GAP).astype(jnp.int32).reshape(1)
    ya = _natten(proj, e_tab, flag_a, _shift_table((bound_a + b_hi).reshape(n_pairs, 2)),
                 batch=batch, seq=seq, n_pairs=n_pairs, q_blk=0, k_blk=blk, v_blk=2 * blk,
                 groups=NATTEN_GROUPS)

    slopes = (LOG2E * 2.0 ** (-8.0 * (np.arange(n_heads_b) + 1.0) / n_heads_b)).astype(np.float32)
    lam_params = jnp.concatenate([lam_q1, lam_k1, lam_q2, lam_k2], axis=0)
    bound_b = _shift_bound(qn_b_q[0], qn_b_k[0], scale * LOG2E)
    flag_b = jnp.logical_not(2.0 * bound_b <= MAX_SHIFT_GAP).astype(jnp.int32).reshape(1)
    shift_b = _shift_table(jnp.broadcast_to(bound_b, (1, 2)))
    yb = _diffattn(proj, vt, flag_b, shift_b, slopes, lam_params, subln_g, batch=batch, seq=seq, n_heads=n_heads_b,
                   q_blk=3 * blk, k_blk=4 * blk, tq=ATT_TILE, tk=ATT_TILE, subs=TILES_PER_STEP)

    x1 = _mix(x2, ya, yb, gate, w_proj_a[0].astype(BF16), w_proj_b[0].astype(BF16),
              w_out[0].astype(BF16), tm=ROW_TILE)

    out = _ffn(x1, p[0].reshape(t, -1), norm_ffn_g, w_up[0].astype(BF16), conv_w[0], conv_b,
               w_down[0].astype(BF16), norm_ple_g, w_ple_gate[0].astype(BF16),
               w_ple_proj[0].astype(BF16), seq=seq, tm=ROW_TILE, subs=TILES_PER_STEP)
    return out.reshape(batch, seq, d)
```

```python
import functools
import math

import jax
import jax.numpy as jnp
import numpy as np
from jax import lax
from jax.experimental import pallas as pl
from jax.experimental.pallas import tpu as pltpu

F32 = jnp.float32
BF16 = jnp.bfloat16

HEAD_DIM = 64
GRID_W = 64
NA_KH = 8
NA_KW = 16
EPS = 1e-6
LAM_INIT = 0.8 - 0.6 * math.exp(-0.3 * 0)
NEG_BIG = -1e30
LOG2E = math.log2(math.e)

LANES = 128
HALO = 8
VMEM_LIMIT = 56 * 1024 * 1024
BF16_EXACT_INT = 256

ROW_TILE = 512
ATT_TILE = 512
TILES_PER_STEP = 2
ATT_TILES_PER_STEP = 4
NATTEN_GROUPS = 8


def _params(n_axes, vmem=VMEM_LIMIT):
    return pltpu.CompilerParams(
        dimension_semantics=("arbitrary",) * n_axes, vmem_limit_bytes=vmem)


def _resident(shape):
    nd = len(shape)
    return pl.BlockSpec(shape, lambda *_: (0,) * nd, pipeline_mode=pl.Buffered(1))


def _rms_scale(x):
    return lax.rsqrt(jnp.mean(x * x, axis=-1, keepdims=True) + EPS)


ROW_GROUP = 4
KEY_ROWS = 12
_GROUP_CLASSES = (
    (0, lambda i: 0),
    (-(NA_KH // 2), lambda i: i - NA_KH // 2),
    (ROW_GROUP - KEY_ROWS, lambda i: ROW_GROUP - NA_KH),
)


def _bias_table_kernel(rpb_ref, e_ref):
    h = pl.program_id(0)
    n_dr = 2 * NA_KH - 1
    n_dc = 2 * NA_KW - 1
    cq = lax.broadcasted_iota(jnp.int32, (GRID_W, LANES), 0)
    lane = lax.broadcasted_iota(jnp.int32, (GRID_W, LANES), 1)
    ck = lane % GRID_W
    upper = lane >= GRID_W
    dc = jnp.clip(ck - cq, -(NA_KW - 1), NA_KW - 1) + (NA_KW - 1)
    cs = jnp.clip(cq - NA_KW // 2, 0, GRID_W - NA_KW)
    inside = (ck >= cs) & (ck < cs + NA_KW)
    neg = jnp.full((GRID_W, LANES), NEG_BIG, F32)
    tiles = []
    for dr in range(n_dr):
        acc = neg
        for d in range(n_dc):
            acc = jnp.where(inside & (dc == d), rpb_ref[h * n_dr + dr, d] * LOG2E, acc)
        tiles.append(acc)

    def half(ws_off, rs_off, i, j):
        row = ws_off + j
        if rs_off <= row < rs_off + NA_KH:
            return tiles[row - i + NA_KH - 1]
        return neg

    for cls, (ws_off, rs_fn) in enumerate(_GROUP_CLASSES):
        for i in range(ROW_GROUP):
            for j in range(0, KEY_ROWS, 2):
                lo = half(ws_off, rs_fn(i), i, j)
                hi = half(ws_off, rs_fn(i), i, j + 1)
                e_ref[0, cls, i * GRID_W:(i + 1) * GRID_W, j * GRID_W:(j + 2) * GRID_W] = (
                    jnp.where(upper, hi, lo))


def _bias_table(rpb):
    n_heads = rpb.shape[0]
    rpb2 = rpb.reshape(n_heads * (2 * NA_KH - 1), 2 * NA_KW - 1)
    shape = (n_heads, len(_GROUP_CLASSES), ROW_GROUP * GRID_W, KEY_ROWS * GRID_W)
    return pl.pallas_call(
        _bias_table_kernel,
        grid=(n_heads,),
        in_specs=[pl.BlockSpec(memory_space=pltpu.SMEM)],
        out_specs=pl.BlockSpec((1,) + shape[1:], lambda h: (h, 0, 0, 0)),
        out_shape=jax.ShapeDtypeStruct(shape, F32),
        compiler_params=_params(1),
        name="bias_table",
    )(rpb2)


CHUNK = 512
SLAB = 256


def _inproj_kernel(x_ref, gmix_ref, w_ref, wg_ref, wvt_ref, bgate_ref, gain_ref, bd_ref,
                   proj_ref, gate_ref, vt_ref, *, d_in, normed):
    x = x_ref[...]
    h = (x * _rms_scale(x) * gmix_ref[...]).astype(BF16)
    vt_ref[0] = lax.dot_general(wvt_ref[...], h, (((1,), (1,)), ((), ())),
                                preferred_element_type=F32).astype(BF16)
    for c in range(d_in // CHUNK):
        p = jnp.dot(h, w_ref[:, c * CHUNK:(c + 1) * CHUNK], preferred_element_type=F32)
        if normed[c]:
            for s in range(CHUNK // SLAB):
                col = slice(c * CHUNK + s * SLAB, c * CHUNK + (s + 1) * SLAB)
                ps = p[:, s * SLAB:(s + 1) * SLAB]
                ms = jnp.dot((ps * ps).astype(BF16), bd_ref[...], preferred_element_type=F32)
                proj_ref[:, col] = (ps * lax.rsqrt(ms + EPS) * gain_ref[:, col]).astype(BF16)
        else:
            proj_ref[:, c * CHUNK:(c + 1) * CHUNK] = p.astype(BF16)
    d_gate = gate_ref.shape[1]
    for c in range(d_gate // CHUNK):
        col = slice(c * CHUNK, (c + 1) * CHUNK)
        z = jnp.dot(h, wg_ref[:, col], preferred_element_type=F32) + bgate_ref[:, col]
        gate_ref[:, col] = jax.nn.sigmoid(z).astype(BF16)


def _inproj(x2, gmix, w_in, w_gate, wvt, b_gate, gain, bd, *, d_in, normed, tm):
    t, d = x2.shape
    d_gate = w_gate.shape[1]
    d_v = wvt.shape[0]
    kern = functools.partial(_inproj_kernel, d_in=d_in, normed=normed)
    return pl.pallas_call(
        kern,
        grid=(t // tm,),
        in_specs=[
            pl.BlockSpec((tm, d), lambda i: (i, 0)),
            _resident(gmix.shape),
            _resident(w_in.shape),
            _resident(w_gate.shape),
            _resident(wvt.shape),
            _resident(b_gate.shape),
            _resident(gain.shape),
            _resident(bd.shape),
        ],
        out_specs=[
            pl.BlockSpec((tm, d_in), lambda i: (i, 0)),
            pl.BlockSpec((tm, d_gate), lambda i: (i, 0)),
            pl.BlockSpec((1, d_v, tm), lambda i: (i, 0, 0)),
        ],
        out_shape=[
            jax.ShapeDtypeStruct((t, d_in), BF16),
            jax.ShapeDtypeStruct((t, d_gate), BF16),
            jax.ShapeDtypeStruct((t // tm, d_v, tm), BF16),
        ],
        compiler_params=_params(1),
        name="inproj",
    )(x2, gmix, w_in, w_gate, wvt, b_gate, gain, bd)


def _natten_kernel(flag_ref, shift_ref, q_ref, k_ref, v_ref, e_ref, o_ref, *, groups, n_rows):
    gblk = pl.program_id(2)
    gq = ROW_GROUP * GRID_W
    win = KEY_ROWS * GRID_W
    lane = lax.broadcasted_iota(jnp.int32, (gq, LANES), 1)
    first = lane < HEAD_DIM
    first_k = lax.broadcasted_iota(jnp.int32, (win, LANES), 1) < HEAD_DIM
    ones = jnp.where((lane[:1] % HEAD_DIM == N_POS_FEATS) | (lane[:1] % HEAD_DIM == N_POS_FEATS + 1),
                     1.0, 0.0).astype(BF16)
    def group(gi, exact_max):
        g = gblk * groups + gi
        r0 = g * ROW_GROUP
        ws = jnp.clip(r0 - NA_KH // 2, 0, n_rows - KEY_ROWS)
        cls = jnp.where(r0 == 0, 0, jnp.where(r0 == n_rows - ROW_GROUP, 2, 1))
        q = q_ref[gi * gq:(gi + 1) * gq, :]
        k0 = pl.multiple_of(ws * GRID_W, GRID_W)
        kw = k_ref[pl.ds(k0, win), :]
        vw = v_ref[pl.ds(k0, win), :]
        kones = jnp.broadcast_to(ones, kw.shape)
        ks = (jnp.where(first_k, kw, kones), jnp.where(first_k, kones, kw))
        f = jnp.zeros_like(q) if exact_max else jnp.broadcast_to(shift_ref[0], q.shape)
        qs = (jnp.where(first, q, f), jnp.where(first, f, q))
        outs = []
        for hh in range(2):
            s = lax.dot_general(qs[hh], ks[hh], (((1,), (1,)), ((), ())),
                                preferred_element_type=F32) + e_ref[hh, cls]
            if exact_max:
                s = s - jnp.max(s, axis=-1, keepdims=True)
            p = jnp.exp2(s)
            l = jnp.sum(p, axis=-1, keepdims=True)
            outs.append(jnp.dot(p.astype(BF16), vw, preferred_element_type=F32) / l)
        o_ref[gi * gq:(gi + 1) * gq, :] = jnp.where(first, outs[0], outs[1]).astype(BF16)

    needs_max = flag_ref[0] != 0

    @pl.when(jnp.logical_not(needs_max))
    def _():
        for gi in range(groups):
            group(gi, False)

    @pl.when(needs_max)
    def _():
        for gi in range(groups):
            group(gi, True)


def _natten(proj, e_tab, flag, shifts, *, batch, seq, n_pairs, q_blk, k_blk, v_blk, groups):
    n_rows = seq // GRID_W
    rb = groups * ROW_GROUP
    assert n_rows >= KEY_ROWS and n_rows % rb == 0
    steps = n_rows // rb
    kern = functools.partial(_natten_kernel, groups=groups, n_rows=n_rows)
    return pl.pallas_call(
        kern,
        grid=(n_pairs, batch, steps),
        in_specs=[
            pl.BlockSpec(memory_space=pltpu.SMEM),
            pl.BlockSpec((1, 1, LANES), lambda hp, b, r: (hp, 0, 0)),
            pl.BlockSpec((rb * GRID_W, LANES), lambda hp, b, r: (b * steps + r, q_blk + hp)),
            pl.BlockSpec((seq, LANES), lambda hp, b, r: (b, k_blk + hp)),
            pl.BlockSpec((seq, LANES), lambda hp, b, r: (b, v_blk + hp)),
            pl.BlockSpec((2,) + e_tab.shape[1:], lambda hp, b, r: (hp, 0, 0, 0)),
        ],
        out_specs=pl.BlockSpec((rb * GRID_W, LANES), lambda hp, b, r: (b * steps + r, hp)),
        out_shape=jax.ShapeDtypeStruct((batch * seq, n_pairs * LANES), BF16),
        compiler_params=_params(3),
        name="natten",
    )(flag, shifts, proj, proj, proj, e_tab)


POS_RADIX = 64
N_POS_FEATS = 8
MAX_SHIFT_GAP = 90.0
BOUND_SLACK = 2.0 ** -6


def _split_bf16_np(x):
    hi = np.asarray(x, np.float32).astype(BF16).astype(np.float32)
    return hi, np.float32(x - hi)


def _pos_feats(seq, slopes):
    pos = np.arange(seq)
    hi = (POS_RADIX * (pos // POS_RADIX)).astype(np.float32)
    lo = (pos % POS_RADIX).astype(np.float32)
    one = np.ones(seq, np.float32)
    zero = np.zeros(seq, np.float32)
    qs, ks = [], []
    for slope in np.asarray(slopes, np.float32):
        s_hi, s_lo = (float(v) for v in _split_bf16_np(np.float32(slope)))
        qs.append(np.stack([hi, hi, lo, lo, s_hi * one, s_lo * one, s_hi * one, s_lo * one,
                            zero, zero], axis=1))
        ks.append(np.stack([s_hi * one, s_lo * one, s_hi * one, s_lo * one, -hi, -hi, -lo, -lo,
                            one, one], axis=1))

    def table(feats):
        half = np.zeros((len(feats), seq, HEAD_DIM), np.float32)
        half[:, :, :feats[0].shape[1]] = np.stack(feats)
        return jnp.asarray(np.concatenate([half, half], axis=2), BF16)

    return table(qs), table(ks)


def _split_bf16(x):
    hi = x.astype(BF16).astype(F32)
    return hi, x - hi


def _shift_feats(shift, tq):
    lane = lax.broadcasted_iota(jnp.int32, (tq, LANES), 1) % HEAD_DIM
    hi, lo = _split_bf16(-shift)
    return jnp.where(lane == N_POS_FEATS, hi, jnp.where(lane == N_POS_FEATS + 1, lo, 0.0))


def _shift_bound(gain_q, gain_k, q_scale):
    bound = HEAD_DIM * jnp.max(jnp.abs(gain_q)) * jnp.max(jnp.abs(gain_k)) * q_scale
    return bound * (1.0 + BOUND_SLACK) + BOUND_SLACK


def _shift_table(shifts):
    n = shifts.shape[0]
    hi, lo = _split_bf16(-shifts)
    tab = jnp.zeros((n, 2, HEAD_DIM), F32)
    tab = tab.at[:, :, N_POS_FEATS].set(hi[:, ::-1]).at[:, :, N_POS_FEATS + 1].set(lo[:, ::-1])
    return tab.reshape(n, 1, LANES).astype(BF16)


def _diffattn_kernel(slopes_ref, flag_ref, lamp_ref, dist_ref, shift_ref, qf_ref, kf_ref,
                     q_ref, k_ref, vt_ref, subg_ref, o_ref, m_sc, *scratch,
                     tq, tk, seq, subs):
    l_scs, acc_scs = scratch[:subs], scratch[subs:]
    h = pl.program_id(1)
    lane = lax.broadcasted_iota(jnp.int32, (tq, LANES), 1)
    first = lane < HEAD_DIM
    n_chunks = seq // tk
    n_slabs = tk // LANES
    diag_bias = dist_ref[...] * (-slopes_ref[h])
    lp = lamp_ref[...]
    lam = (jnp.exp(jnp.sum(lp[0:1] * lp[1:2], axis=-1, keepdims=True))
           - jnp.exp(jnp.sum(lp[2:3] * lp[3:4], axis=-1, keepdims=True)) + LAM_INIT)

    def tile(u, exact_max):
        l_sc, acc_sc = l_scs[u], acc_scs[u]
        qi = pl.program_id(2) * subs + u
        rows = slice(u * tq, (u + 1) * tq)
        q = q_ref[rows, :]
        posf = qf_ref[0, rows, :].astype(F32)

        def operands(feats):
            f = feats.astype(BF16)
            return jnp.where(first, q, f), jnp.where(first, f, q)

        def for_chunks(shiftf, consume, unroll, keys_major):
            q_diag = operands(shiftf)
            q_after = operands(posf + shiftf)
            q_before = operands(shiftf - posf)

            def chunk(c, qs, bias):
                k0 = pl.multiple_of(c * tk, tk)
                kc = k_ref[pl.ds(k0, tk), :]
                kf = kf_ref[0, pl.ds(k0, tk), :]
                ks = (jnp.where(first, kc, kf), jnp.where(first, kf, kc))
                ss = []
                for mi in range(2):
                    lhs, rhs = (ks[mi], qs[mi]) if keys_major else (qs[mi], ks[mi])
                    s = lax.dot_general(lhs, rhs, (((1,), (1,)), ((), ())),
                                        preferred_element_type=F32)
                    ss.append(s if bias is None else s + bias)
                consume(c, ss)

            chunk(qi, q_diag, diag_bias)

            def body(d, carry):
                c = lax.rem(qi + d, n_chunks)
                after = c > qi
                qs = tuple(jnp.where(after, q_after[mi], q_before[mi]) for mi in range(2))
                chunk(c, qs, None)
                return carry

            lax.fori_loop(1, n_chunks, body, 0, unroll=unroll)

        def attend(shiftf, unroll):
            l_sc[...] = jnp.zeros(l_sc.shape, F32)
            acc_sc[...] = jnp.zeros(acc_sc.shape, F32)

            def consume(c, ss):
                vt = vt_ref[c]
                for mi in range(2):
                    p = jnp.exp2(ss[mi])
                    l_sc[mi] += jnp.sum(p.reshape(tk // HALO, HALO, tq), axis=0)
                    acc_sc[mi] += jnp.dot(vt, p.astype(BF16), preferred_element_type=F32)

            for_chunks(shiftf, consume, unroll, True)

        if exact_max:
            m_sc[...] = jnp.full(m_sc.shape, NEG_BIG, F32)

            def consume(c, ss):
                for mi in range(2):
                    m = m_sc[mi]
                    for j in range(n_slabs):
                        m = jnp.maximum(m, ss[mi][:, j * LANES:(j + 1) * LANES])
                    m_sc[mi] = m

            for_chunks(jnp.zeros((tq, LANES), F32), consume, False, False)
            rowmax = [jnp.max(m_sc[mi], axis=-1, keepdims=True) for mi in range(2)]
            attend(_shift_feats(jnp.where(first, rowmax[1], rowmax[0]), tq), False)
        else:
            attend(jnp.broadcast_to(shift_ref[0].astype(F32), (tq, LANES)), True)

        ys = [acc_sc[mi] / jnp.sum(l_sc[mi], axis=0, keepdims=True) for mi in range(2)]
        yt = ys[0] - lam * ys[1]
        yt = yt * lax.rsqrt(jnp.mean(yt * yt, axis=0, keepdims=True) + EPS)
        o_ref[rows, :] = (yt.T * subg_ref[...] * (1.0 - LAM_INIT)).astype(BF16)

    needs_max = flag_ref[0] != 0

    @pl.when(jnp.logical_not(needs_max))
    def _():
        for u in range(subs):
            tile(u, False)

    @pl.when(needs_max)
    def _():
        for u in range(subs):
            tile(u, True)


def _diffattn(proj, vt, flag, shift, slopes, lam_params, subg, *, batch, seq, n_heads, q_blk,
              k_blk, tq, tk, subs):
    rows = subs * tq
    steps = seq // rows
    assert tq == tk and seq % rows == 0 and seq <= POS_RADIX * BF16_EXACT_INT
    assert vt.shape == (batch * seq // tk, n_heads * LANES, tk)
    qfeat, kfeat = _pos_feats(seq, slopes)
    idx = np.arange(tq)
    dist = jnp.asarray(np.abs(idx[:, None] - idx[None, :]), F32)
    kern = functools.partial(_diffattn_kernel, tq=tq, tk=tk, seq=seq, subs=subs)
    return pl.pallas_call(
        kern,
        grid=(batch, n_heads, steps),
        in_specs=[
            pl.BlockSpec(memory_space=pltpu.SMEM),
            pl.BlockSpec(memory_space=pltpu.SMEM),
            pl.BlockSpec(lam_params.shape, lambda b, h, i: (0, 0)),
            _resident(dist.shape),
            _resident(shift.shape),
            pl.BlockSpec((1, rows, LANES), lambda b, h, i: (h, i, 0)),
            pl.BlockSpec((1, seq, LANES), lambda b, h, i: (h, 0, 0)),
            pl.BlockSpec((rows, LANES), lambda b, h, i: (b * steps + i, q_blk + h)),
            pl.BlockSpec((seq, LANES), lambda b, h, i: (b, k_blk + h)),
            pl.BlockSpec((seq // tk, LANES, tk), lambda b, h, i: (b, h, 0)),
            pl.BlockSpec(subg.shape, lambda b, h, i: (0, 0)),
        ],
        out_specs=pl.BlockSpec((rows, LANES), lambda b, h, i: (b * steps + i, h)),
        out_shape=jax.ShapeDtypeStruct((batch * seq, n_heads * LANES), BF16),
        scratch_shapes=(
            [pltpu.VMEM((2, tq, LANES), F32)]
            + [pltpu.VMEM((2, HALO, tq), F32)] * subs
            + [pltpu.VMEM((2, LANES, tq), F32)] * subs),
        compiler_params=_params(3),
        name="diffattn",
    )(jnp.asarray(slopes, F32), flag, lam_params, dist, shift, qfeat, kfeat,
      proj, proj, vt, subg)


def _mix_kernel(x_ref, ya_ref, yb_ref, g_ref, wa_ref, wb_ref, wo_ref, o_ref):
    d = x_ref.shape[1]
    pa = jnp.dot(ya_ref[...], wa_ref[...], preferred_element_type=F32)
    pb = jnp.dot(yb_ref[...], wb_ref[...], preferred_element_type=F32)
    mixed = g_ref[:, :d].astype(F32) * pa + g_ref[:, d:].astype(F32) * pb
    o_ref[...] = x_ref[...] + jnp.dot(mixed.astype(BF16), wo_ref[...],
                                      preferred_element_type=F32)


def _mix(x2, ya, yb, gate, wa, wb, wo, *, tm):
    t, d = x2.shape
    return pl.pallas_call(
        _mix_kernel,
        grid=(t // tm,),
        in_specs=[
            pl.BlockSpec((tm, d), lambda i: (i, 0)),
            pl.BlockSpec((tm, ya.shape[1]), lambda i: (i, 0)),
            pl.BlockSpec((tm, yb.shape[1]), lambda i: (i, 0)),
            pl.BlockSpec((tm, gate.shape[1]), lambda i: (i, 0)),
            _resident(wa.shape),
            _resident(wb.shape),
            _resident(wo.shape),
        ],
        out_specs=pl.BlockSpec((tm, d), lambda i: (i, 0)),
        out_shape=jax.ShapeDtypeStruct((t, d), F32),
        compiler_params=_params(1),
        name="mix",
    )(x2, ya, yb, gate, wa, wb, wo)


FF_CHUNK = 256
FF_SLOTS = 2


def _ffn_kernel(x_ref, xp_ref, xn_ref, p_ref, gffn_ref, wup_ref, cw_ref, cb_ref, wdn_ref,
                gple_ref, wpg_ref, wpp_ref, o_ref, *scratch, tm, seq, d_ff, subs):
    per = 2 + 2 * FF_SLOTS
    i = pl.program_id(0)
    steps_per_seq = seq // (subs * tm)
    pos = i % steps_per_seq
    keep_prev = (pos != 0).astype(F32)
    keep_next = (pos != steps_per_seq - 1).astype(F32)
    g = gffn_ref[...]
    n_chunks = d_ff // FF_CHUNK

    def norm(v):
        return v * _rms_scale(v) * g

    def cols(c):
        return slice(c * FF_CHUNK, (c + 1) * FF_CHUNK)

    def gelu_tanh(v):
        c = math.sqrt(2.0 / math.pi)
        return v * (0.5 + 0.5 * jnp.tanh(v * (c + (c * 0.044715) * (v * v))))

    def conv(u_ref, col):
        w = cw_ref[:, col]
        u = u_ref[...]
        rows = u.shape[0]
        prev = pltpu.roll(u, 1, 0)[HALO:HALO + tm]
        nxt = pltpu.roll(u, rows - 1, 0)[HALO:HALO + tm]
        return w[0:1] * prev + w[1:2] * u[HALO:HALO + tm] + w[2:3] * nxt + cb_ref[:, col]

    def hidden(u):
        a_sc, h_sc = scratch[u * per], scratch[u * per + 1]
        uv_sc = scratch[u * per + 2:u * per + 2 + FF_SLOTS]
        ug_sc = scratch[u * per + 2 + FF_SLOTS:(u + 1) * per]
        r0 = u * tm
        before = (norm(xp_ref[...]) * keep_prev if u == 0
                  else norm(x_ref[r0 - HALO:r0, :]))
        after = (norm(xn_ref[...]) * keep_next if u == subs - 1
                 else norm(x_ref[r0 + tm:r0 + tm + HALO, :]))
        h_sc[...] = jnp.concatenate([before, norm(x_ref[r0:r0 + tm, :]), after],
                                    axis=0).astype(BF16)

        def up(c, slot):
            uv_sc[slot][...] = jnp.dot(h_sc[...], wup_ref[:, cols(c)],
                                       preferred_element_type=F32)
            ug_sc[slot][...] = jnp.dot(h_sc[...], wup_ref[:, cols(n_chunks + c)],
                                       preferred_element_type=F32)

        def act(c, slot):
            uv = conv(uv_sc[slot], cols(c))
            ug = conv(ug_sc[slot], cols(n_chunks + c))
            a_sc[:, cols(c)] = (gelu_tanh(ug) * uv).astype(BF16)

        up(0, 0)
        for c in range(n_chunks):
            if c + 1 < n_chunks:
                up(c + 1, (c + 1) % FF_SLOTS)
            act(c, c % FF_SLOTS)

    def finish(u):
        a_sc = scratch[u * per]
        rows = slice(u * tm, (u + 1) * tm)
        x2 = x_ref[rows, :] + jnp.dot(a_sc[...], wdn_ref[...], preferred_element_type=F32)
        h3 = (x2 * _rms_scale(x2) * gple_ref[...]).astype(BF16)
        pg = jax.nn.sigmoid(jnp.dot(h3, wpg_ref[...], preferred_element_type=F32))
        pp = jnp.dot(p_ref[rows, :].astype(BF16), wpp_ref[...], preferred_element_type=F32)
        o_ref[rows, :] = x2 + pg * pp

    for u in range(subs):
        hidden(u)
    for u in range(subs):
        finish(u)


def _ffn(x1, p2, gffn, wup, cw, cb, wdn, gple, wpg, wpp, *, seq, tm, subs):
    t, d = x1.shape
    d_ff = wdn.shape[0]
    rows = subs * tm
    assert seq % rows == 0 and tm % HALO == 0 and d_ff % FF_CHUNK == 0
    hb = rows // HALO
    last = t // HALO - 1
    kern = functools.partial(_ffn_kernel, tm=tm, seq=seq, d_ff=d_ff, subs=subs)
    tile_scratch = ([pltpu.VMEM((tm, d_ff), BF16), pltpu.VMEM((tm + 2 * HALO, d), BF16)]
                    + [pltpu.VMEM((tm + 2 * HALO, FF_CHUNK), F32)] * (2 * FF_SLOTS))
    return pl.pallas_call(
        kern,
        grid=(t // rows,),
        in_specs=[
            pl.BlockSpec((rows, d), lambda i: (i, 0)),
            pl.BlockSpec((HALO, d), lambda i: (jnp.maximum(i * hb - 1, 0), 0)),
            pl.BlockSpec((HALO, d), lambda i: (jnp.minimum((i + 1) * hb, last), 0)),
            pl.BlockSpec((rows, p2.shape[1]), lambda i: (i, 0)),
            _resident(gffn.shape),
            _resident(wup.shape),
            _resident(cw.shape),
            _resident(cb.shape),
            _resident(wdn.shape),
            _resident(gple.shape),
            _resident(wpg.shape),
            _resident(wpp.shape),
        ],
        out_specs=pl.BlockSpec((rows, d), lambda i: (i, 0)),
        out_shape=jax.ShapeDtypeStruct((t, d), F32),
        scratch_shapes=tile_scratch * subs,
        compiler_params=_params(1),
        name="ffn",
    )(x1, x1, x1, p2, gffn, wup, cw, cb, wdn, gple, wpg, wpp)


def kernel(x, p, norm_mix_g, w_in, qn_a_q, qn_a_k, rpb, qn_b_q, qn_b_k, lam_q1, lam_k1, lam_q2, lam_k2, subln_g, w_proj_a, w_proj_b, w_gate, b_gate, w_out, norm_ffn_g, w_up, conv_w, conv_b, w_down, norm_ple_g, w_ple_gate, w_ple_proj):
    batch, seq, d = x.shape
    depth = p.shape[0]
    assert depth == 1
    t = batch * seq
    width_a = w_proj_a.shape[1]
    width_b = w_proj_b.shape[1]
    n_heads_a = width_a // HEAD_DIM
    n_heads_b = width_b // (2 * HEAD_DIM)
    d_in = 3 * width_a + 2 * width_b
    assert width_a == CHUNK and width_b == CHUNK

    scale = 1.0 / math.sqrt(HEAD_DIM)
    ones = jnp.ones((CHUNK,), F32)
    gain = jnp.concatenate([
        jnp.tile(qn_a_q[0], n_heads_a) * (scale * LOG2E), jnp.tile(qn_a_k[0], n_heads_a), ones,
        jnp.tile(qn_b_q[0], 2 * n_heads_b) * (scale * LOG2E), jnp.tile(qn_b_k[0], 2 * n_heads_b),
    ])[None, :]
    normed = (True, True, False, True, True)
    bd = jnp.asarray(np.kron(np.eye(SLAB // HEAD_DIM), np.full((HEAD_DIM, HEAD_DIM), 1.0 / HEAD_DIM)),
                     BF16)
    wvt = w_in[0][:, d_in:].T.astype(BF16)

    x2 = x.reshape(t, d)
    proj, gate, vt = _inproj(x2, norm_mix_g, w_in[0].astype(BF16), w_gate[0].astype(BF16), wvt,
                             b_gate, gain, bd, d_in=d_in, normed=normed, tm=ROW_TILE)

    e_tab = _bias_table(rpb[0])
    blk = CHUNK // LANES
    n_pairs = n_heads_a // 2
    bound_a = _shift_bound(qn_a_q[0], qn_a_k[0], scale * LOG2E)
    b_hi = jnp.max(rpb[0], axis=(1, 2)) * LOG2E
    b_self = rpb[0][:, NA_KH - 1, NA_KW - 1] * LOG2E
    gap_a = 2.0 * bound_a + jnp.max(b_hi - b_self)
    flag_a = jnp.logical_not(gap_a <= MAX_SHIFT_GAP).astype(jnp.int32).reshape(1)
    ya = _natten(proj, e_tab, flag_a, _shift_table((bound_a + b_hi).reshape(n_pairs, 2)),
                 batch=batch, seq=seq, n_pairs=n_pairs, q_blk=0, k_blk=blk, v_blk=2 * blk,
                 groups=NATTEN_GROUPS)

    slopes = (LOG2E * 2.0 ** (-8.0 * (np.arange(n_heads_b) + 1.0) / n_heads_b)).astype(np.float32)
    lam_params = jnp.concatenate([lam_q1, lam_k1, lam_q2, lam_k2], axis=0)
    bound_b = _shift_bound(qn_b_q[0], qn_b_k[0], scale * LOG2E)
    flag_b = jnp.logical_not(2.0 * bound_b <= MAX_SHIFT_GAP).astype(jnp.int32).reshape(1)
    shift_b = _shift_table(jnp.broadcast_to(bound_b, (1, 2)))
    yb = _diffattn(proj, vt, flag_b, shift_b, slopes, lam_params, subln_g, batch=batch, seq=seq, n_heads=n_heads_b,
                   q_blk=3 * blk, k_blk=4 * blk, tq=ATT_TILE, tk=ATT_TILE,
                   subs=ATT_TILES_PER_STEP)

    x1 = _mix(x2, ya, yb, gate, w_proj_a[0].astype(BF16), w_proj_b[0].astype(BF16),
              w_out[0].astype(BF16), tm=ROW_TILE)

    out = _ffn(x1, p[0].reshape(t, -1), norm_ffn_g, w_up[0].astype(BF16), conv_w[0], conv_b,
               w_down[0].astype(BF16), norm_ple_g, w_ple_gate[0].astype(BF16),
               w_ple_proj[0].astype(BF16), seq=seq, tm=ROW_TILE, subs=TILES_PER_STEP)
    return out.reshape(batch, seq, d)
```
